```python
import math
import jax, jax.numpy as jnp
from jax import lax
import numpy as np

D_MODEL = 1024
BATCH = 16
SEQ = 4096
DEPTH = 1

CHUNK = 64
D_MIX = D_MODEL
LRU_WIDTH = D_MIX // 2
LRU_BLOCKS = 8
LRU_BLOCK = LRU_WIDTH // LRU_BLOCKS
CONV_WIDTH = 4
LRU_C = 8.0
SB_WIDTH = D_MIX - LRU_WIDTH
SB_HEADS = 8
SB_HEAD_DIM = SB_WIDTH // SB_HEADS
Q_BLOCK = 128
D_FF = 4 * D_MODEL
IN_COLS = 2 * LRU_WIDTH + 3 * SB_WIDTH
EPS = 1e-6

kernel_name = "hymba_rglru_stickbreaking_block"


def _rmsnorm(x, g):
    xf = x.astype(jnp.float32)
    y = xf * lax.rsqrt(jnp.mean(xf * xf, axis=-1, keepdims=True) + EPS)
    return (y * g.astype(jnp.float32)).astype(x.dtype)


def _causal_depthwise_conv(u, w, b):
    rhs = w[:, None, :].astype(u.dtype)
    y = lax.conv_general_dilated(
        u, rhs, window_strides=(1,), padding=[(CONV_WIDTH - 1, 0)],
        dimension_numbers=("NWC", "WIO", "NWC"), feature_group_count=u.shape[-1])
    return y + b.astype(u.dtype)


def _rg_lru(u, w_a, b_a, w_x, b_x, lam):
    B, S, _ = u.shape
    uf = u.astype(jnp.float32)
    ub = uf.reshape(B, S, LRU_BLOCKS, LRU_BLOCK)
    r = jax.nn.sigmoid(jnp.einsum('bsgi,gij->bsgj', ub, w_a.astype(jnp.float32))
                       + b_a.astype(jnp.float32)).reshape(B, S, LRU_WIDTH)
    i = jax.nn.sigmoid(jnp.einsum('bsgi,gij->bsgj', ub, w_x.astype(jnp.float32))
                       + b_x.astype(jnp.float32)).reshape(B, S, LRU_WIDTH)
    log_a = -LRU_C * r * jax.nn.softplus(-lam.astype(jnp.float32))
    a = jnp.exp(log_a)
    bterm = jnp.sqrt(-jnp.expm1(2.0 * log_a)) * (i * uf)

    def combine(left, right):
        a1, b1 = left
        a2, b2 = right
        return a1 * a2, a2 * b1 + b2

    _, h = lax.associative_scan(combine, (a, bterm), axis=1)
    return h.astype(u.dtype)


def _stick_breaking(q, k, v):
    B, S, H, Dh = q.shape
    scale = 1.0 / math.sqrt(Dh)
    qh = jnp.transpose(q, (0, 2, 1, 3)).astype(jnp.float32)
    kh = jnp.transpose(k, (0, 2, 1, 3)).astype(jnp.float32)
    vh = jnp.transpose(v, (0, 2, 1, 3)).astype(jnp.float32)
    outs = []
    for blk in range(S // Q_BLOCK):
        start = blk * Q_BLOCK
        end = start + Q_BLOCK
        qb = qh[:, :, start:end]
        kb = kh[:, :, :end]
        vb = vh[:, :, :end]
        z = jnp.einsum('bhqd,bhkd->bhqk', qb, kb) * scale
        t_idx = jnp.arange(start, end)[:, None]
        s_idx = jnp.arange(end)[None, :]
        mask = s_idx < t_idx
        log_1m_beta = jnp.where(mask, jax.nn.log_sigmoid(-z), 0.0)
        surv = lax.cumsum(log_1m_beta, axis=3, reverse=True) - log_1m_beta
        att = jnp.where(mask, jnp.exp(jax.nn.log_sigmoid(z) + surv), 0.0)
        outs.append(jnp.einsum('bhqk,bhkd->bhqd', att, vb))
    o = jnp.concatenate(outs, axis=2)
    return jnp.transpose(o, (0, 2, 1, 3)).reshape(B, S, H * Dh).astype(q.dtype)


def setup_inputs(seed: int = 0) -> dict:
    key = jax.random.key(seed)
    ks = jax.random.split(key, 20)
    f32 = jnp.float32
    x = jax.random.normal(ks[0], (BATCH, SEQ, D_MODEL), f32)
    norm1_g = 1.0 + 0.01 * jax.random.normal(ks[1], (DEPTH, D_MODEL), f32)
    w_in = jax.random.normal(ks[2], (DEPTH, D_MODEL, IN_COLS), f32) * D_MODEL ** -0.5
    conv_w = jax.random.normal(ks[3], (DEPTH, CONV_WIDTH, LRU_WIDTH), f32) * CONV_WIDTH ** -0.5
    conv_b = 0.01 * jax.random.normal(ks[4], (DEPTH, LRU_WIDTH), f32)
    lru_w_a = jax.random.normal(ks[5], (DEPTH, LRU_BLOCKS, LRU_BLOCK, LRU_BLOCK), f32) * LRU_BLOCK ** -0.5
    lru_b_a = 0.01 * jax.random.normal(ks[6], (DEPTH, LRU_BLOCKS, LRU_BLOCK), f32)
    lru_w_x = jax.random.normal(ks[7], (DEPTH, LRU_BLOCKS, LRU_BLOCK, LRU_BLOCK), f32) * LRU_BLOCK ** -0.5
    lru_b_x = 0.01 * jax.random.normal(ks[8], (DEPTH, LRU_BLOCKS, LRU_BLOCK), f32)
    u = jax.random.uniform(ks[9], (DEPTH, LRU_WIDTH), f32, 0.9, 0.999)
    a0 = u ** (1.0 / LRU_C)
    lru_lambda = jnp.log(a0) - jnp.log1p(-a0)
    lru_out_g = 1.0 + 0.01 * jax.random.normal(ks[10], (DEPTH, LRU_WIDTH), f32)
    sb_out_g = 1.0 + 0.01 * jax.random.normal(ks[11], (DEPTH, SB_WIDTH), f32)
    w_out = jax.random.normal(ks[12], (DEPTH, D_MIX, D_MODEL), f32) * D_MIX ** -0.5
    norm2_g = 1.0 + 0.01 * jax.random.normal(ks[13], (DEPTH, D_MODEL), f32)
    w_up = jax.random.normal(ks[14], (DEPTH, D_MODEL, D_FF), f32) * D_MODEL ** -0.5
    w_down = jax.random.normal(ks[15], (DEPTH, D_FF, D_MODEL), f32) * D_FF ** -0.5
    final_g = 1.0 + 0.01 * jax.random.normal(ks[16], (D_MODEL,), f32)
    return {"x": x, "norm1_g": norm1_g, "w_in": w_in, "conv_w": conv_w, "conv_b": conv_b,
            "lru_w_a": lru_w_a, "lru_b_a": lru_b_a, "lru_w_x": lru_w_x, "lru_b_x": lru_b_x,
            "lru_lambda": lru_lambda, "lru_out_g": lru_out_g, "sb_out_g": sb_out_g,
            "w_out": w_out, "norm2_g": norm2_g, "w_up": w_up, "w_down": w_down,
            "final_g": final_g}


def reference(x, norm1_g, w_in, conv_w, conv_b, lru_w_a, lru_b_a, lru_w_x, lru_b_x,
              lru_lambda, lru_out_g, sb_out_g, w_out, norm2_g, w_up, w_down, final_g):
    B, S, _ = x.shape
    split_pts = [LRU_WIDTH, 2 * LRU_WIDTH, 2 * LRU_WIDTH + SB_WIDTH, 2 * LRU_WIDTH + 2 * SB_WIDTH]
    h = x
    for layer in range(DEPTH):
        xn = _rmsnorm(h, norm1_g[layer])
        proj = xn @ w_in[layer].astype(xn.dtype)
        x_lru, g_lru, q, k, v = jnp.split(proj, split_pts, axis=-1)
        c = _causal_depthwise_conv(x_lru, conv_w[layer], conv_b[layer])
        y_lru = _rg_lru(c, lru_w_a[layer], lru_b_a[layer], lru_w_x[layer], lru_b_x[layer],
                        lru_lambda[layer]) * jax.nn.gelu(g_lru, approximate=True)
        y_sb = _stick_breaking(q.reshape(B, S, SB_HEADS, SB_HEAD_DIM),
                               k.reshape(B, S, SB_HEADS, SB_HEAD_DIM),
                               v.reshape(B, S, SB_HEADS, SB_HEAD_DIM))
        mix = jnp.concatenate([_rmsnorm(y_lru, lru_out_g[layer]),
                               _rmsnorm(y_sb, sb_out_g[layer])], axis=-1)
        h = h + mix @ w_out[layer].astype(mix.dtype)
        hn = _rmsnorm(h, norm2_g[layer])
        up = jax.nn.relu(hn @ w_up[layer].astype(hn.dtype))
        h = h + (up * up) @ w_down[layer].astype(up.dtype)
    return _rmsnorm(h, final_g)
```

```python
import functools
import math

import jax
import jax.numpy as jnp
from jax import lax
from jax.experimental import pallas as pl
from jax.experimental.pallas import tpu as pltpu

D_MODEL = 1024
LRU_WIDTH = 512
LRU_BLOCKS = 8
LRU_BLOCK = LRU_WIDTH // LRU_BLOCKS
CONV_WIDTH = 4
LRU_C = 8.0
SB_WIDTH = 512
SB_HEADS = 8
SB_HEAD_DIM = SB_WIDTH // SB_HEADS
D_FF = 4 * D_MODEL
EPS = 1e-6

F32 = jnp.float32
BF16 = jnp.bfloat16

LANES = 128
SUBLANES = 8
VMEM_LIMIT_BYTES = 56 * 1024 * 1024

TT = 32
HALO = SUBLANES
TQ = 256
TK = 128
HEADS_PER_STEP = LANES // SB_HEAD_DIM
TM = 512
TF = 1024


def _rms(x, g):
    ms = jnp.mean(x * x, axis=-1, keepdims=True)
    return x * lax.rsqrt(ms + EPS) * g


def _sigmoid(x):
    return 1.0 / (1.0 + jnp.exp(-x))


def _gelu_tanh(x):
    c = math.sqrt(2.0 / math.pi)
    return 0.5 * x * (1.0 + jnp.tanh(c * (x + 0.044715 * (x * x * x))))


def _mix_in_kernel(x_ref, g1_ref, win_ref, cw_ref, cb_ref, wa_ref, ba_ref, wx_ref, bx_ref,
                   lam_ref, gout_ref,
                   ylru_ref, q_ref, k_ref, v_ref,
                   xl_ext, h_state, a_s, b_s, hs_s):
    nb = x_ref.shape[0]
    rows = nb * TT

    @pl.when(pl.program_id(0) == 0)
    def _():
        xl_ext[:, 0:HALO, :] = jnp.zeros((nb, HALO, LRU_WIDTH), F32)
        h_state[...] = jnp.zeros_like(h_state)

    x = x_ref[...].reshape(rows, D_MODEL)
    xn = _rms(x, g1_ref[...]).astype(BF16)

    def proj(lo, width):
        return jnp.dot(xn, win_ref[:, lo:lo + width], preferred_element_type=F32)

    xl = proj(0, LRU_WIDTH)
    gate = proj(LRU_WIDTH, LRU_WIDTH)
    base = 2 * LRU_WIDTH
    q_ref[...] = (proj(base, SB_WIDTH) * (1.0 / math.sqrt(SB_HEAD_DIM))).astype(BF16).reshape(nb, TT, SB_WIDTH)
    k_ref[...] = proj(base + SB_WIDTH, SB_WIDTH).astype(BF16).reshape(nb, TT, SB_WIDTH)
    v_ref[...] = proj(base + 2 * SB_WIDTH, SB_WIDTH).astype(BF16).reshape(nb, TT, SB_WIDTH)

    xl_ext[:, HALO:HALO + TT, :] = xl.reshape(nb, TT, LRU_WIDTH)
    c = jnp.broadcast_to(cb_ref[...].reshape(1, 1, LRU_WIDTH), (nb, TT, LRU_WIDTH))
    for kk in range(CONV_WIDTH):
        start = HALO - (CONV_WIDTH - 1) + kk
        c = c + cw_ref[kk:kk + 1, :].reshape(1, 1, LRU_WIDTH) * xl_ext[:, start:start + TT, :]
    xl_ext[:, 0:HALO, :] = xl_ext[:, TT:TT + HALO, :]
    c = c.reshape(rows, LRU_WIDTH)

    cb16 = c.astype(BF16)
    r = _sigmoid(jnp.dot(cb16, wa_ref[...], preferred_element_type=F32) + ba_ref[...])
    i = _sigmoid(jnp.dot(cb16, wx_ref[...], preferred_element_type=F32) + bx_ref[...])
    lam = lam_ref[...]
    softplus_neg_lam = jnp.maximum(-lam, 0.0) + jnp.log(1.0 + jnp.exp(-jnp.abs(lam)))
    log_a = (-LRU_C * softplus_neg_lam) * r
    a = jnp.exp(log_a)
    bterm = jnp.sqrt(-jnp.tanh(log_a) * (a * a + 1.0)) * (i * c)
    n_groups = LRU_WIDTH // LANES
    for g in range(n_groups):
        a_s[g] = a[:, g * LANES:(g + 1) * LANES]
        b_s[g] = bterm[:, g * LANES:(g + 1) * LANES]

    for g in range(n_groups):
        h = h_state[g]
        for t in range(TT):
            h = a_s[g, pl.ds(t, nb, stride=TT), :] * h + b_s[g, pl.ds(t, nb, stride=TT), :]
            hs_s[g, pl.ds(t, nb, stride=TT), :] = h
        h_state[g] = h

    hs = jnp.concatenate([hs_s[g] for g in range(n_groups)], axis=1)
    y = hs * _gelu_tanh(gate)
    ylru_ref[...] = _rms(y, gout_ref[...]).astype(BF16).reshape(nb, TT, LRU_WIDTH)


def _mix_in(x, g1, win, cw, cb, wa, ba, wx, bx, lam, gout):
    nb, seq, _ = x.shape
    rows = nb * TT
    const2 = lambda j: (0, 0)
    out_sds = jax.ShapeDtypeStruct((nb, seq, LRU_WIDTH), BF16)
    out_spec = pl.BlockSpec((nb, TT, LRU_WIDTH), lambda j: (0, j, 0))
    return pl.pallas_call(
        _mix_in_kernel,
        grid=(seq // TT,),
        in_specs=[
            pl.BlockSpec((nb, TT, D_MODEL), lambda j: (0, j, 0)),
            pl.BlockSpec((1, D_MODEL), const2),
            pl.BlockSpec(win.shape, const2),
            pl.BlockSpec(cw.shape, const2),
            pl.BlockSpec((1, LRU_WIDTH), const2),
            pl.BlockSpec(wa.shape, const2),
            pl.BlockSpec((1, LRU_WIDTH), const2),
            pl.BlockSpec(wx.shape, const2),
            pl.BlockSpec((1, LRU_WIDTH), const2),
            pl.BlockSpec((1, LRU_WIDTH), const2),
            pl.BlockSpec((1, LRU_WIDTH), const2),
        ],
        out_specs=[out_spec, out_spec, out_spec, out_spec],
        out_shape=[out_sds, out_sds, out_sds, out_sds],
        scratch_shapes=[
            pltpu.VMEM((nb, HALO + TT, LRU_WIDTH), F32),
            pltpu.VMEM((LRU_WIDTH // LANES, nb, LANES), F32),
            pltpu.VMEM((LRU_WIDTH // LANES, rows, LANES), F32),
            pltpu.VMEM((LRU_WIDTH // LANES, rows, LANES), F32),
            pltpu.VMEM((LRU_WIDTH // LANES, rows, LANES), F32),
        ],
        compiler_params=pltpu.CompilerParams(
            dimension_semantics=("arbitrary",), vmem_limit_bytes=VMEM_LIMIT_BYTES),
        name="mix_in",
    )(x, g1, win, cw, cb, wa, ba, wx, bx, lam, gout)


def _attn_kernel(q_ref, k_ref, v_ref, o_ref, vt_s):
    qb = pl.program_id(2)
    n_kblocks = vt_s.shape[0]

    @pl.when(qb == 0)
    def _():
        for cblk in range(n_kblocks):
            blk = v_ref[0, cblk * TK:(cblk + 1) * TK, :].astype(F32)
            vt_s[cblk] = blk.T.astype(BF16)

    q = q_ref[0]
    lane = lax.broadcasted_iota(jnp.int32, (TQ, LANES), 1)
    key_i = lax.broadcasted_iota(jnp.int32, (TK, TQ), 0)
    qry_i = lax.broadcasted_iota(jnp.int32, (TK, TQ), 1)
    us = lax.broadcasted_iota(jnp.int32, (TK, 2 * TK), 0)
    uj = lax.broadcasted_iota(jnp.int32, (TK, 2 * TK), 1)
    strict_upper2 = jnp.where((uj % TK) > us, 1.0, 0.0).astype(BF16)

    def tile(kb, qh, head, carry, acc, mask):
        kblk = k_ref[0, pl.ds(pl.multiple_of(kb * TK, TK), TK), :]
        z = lax.dot_general(kblk, qh, (((1,), (1,)), ((), ())), preferred_element_type=F32)
        lg = jnp.log(1.0 + jnp.exp(-jnp.abs(z)))
        log_beta = jnp.minimum(z, 0.0) - lg
        log_1m_beta = log_beta - z
        if mask is not None:
            log_1m_beta = jnp.where(mask, log_1m_beta, 0.0)
        hi = log_1m_beta.astype(BF16)
        lo = (log_1m_beta - hi.astype(F32)).astype(BF16)
        surv = jnp.dot(strict_upper2, jnp.concatenate([hi, lo], axis=0), preferred_element_type=F32)
        att = jnp.exp(log_beta + surv + carry)
        if mask is not None:
            att = jnp.where(mask, att, 0.0)
        vt = vt_s[kb, head * SB_HEAD_DIM:(head + 1) * SB_HEAD_DIM, :]
        acc = acc + jnp.dot(vt, att.astype(BF16), preferred_element_type=F32)
        carry = carry + surv[0:1, :] + log_1m_beta[0:1, :]
        return carry, acc

    outs = []
    n_diag = TQ // TK
    for head in range(HEADS_PER_STEP):
        in_head = (lane >= head * SB_HEAD_DIM) & (lane < (head + 1) * SB_HEAD_DIM)
        qh = jnp.where(in_head, q, jnp.zeros_like(q))
        carry = jnp.zeros((1, TQ), F32)
        acc = jnp.zeros((SB_HEAD_DIM, TQ), F32)
        for d in reversed(range(n_diag)):
            mask = (key_i + d * TK) < qry_i
            carry, acc = tile(qb * n_diag + d, qh, head, carry, acc, mask)

        def body(i, state, qh=qh, head=head):
            carry, acc = state
            return tile(qb * n_diag - 1 - i, qh, head, carry, acc, None)

        carry, acc = lax.fori_loop(0, qb * n_diag, body, (carry, acc))
        outs.append(acc)
    o_ref[0] = jnp.concatenate(outs, axis=0).T


def _attn(q, k, v):
    nb, seq, _ = q.shape
    n_slabs = SB_WIDTH // LANES
    return pl.pallas_call(
        _attn_kernel,
        grid=(nb, n_slabs, seq // TQ),
        in_specs=[
            pl.BlockSpec((1, TQ, LANES), lambda b, s, i: (b, i, s)),
            pl.BlockSpec((1, seq, LANES), lambda b, s, i: (b, 0, s)),
            pl.BlockSpec((1, seq, LANES), lambda b, s, i: (b, 0, s)),
        ],
        out_specs=pl.BlockSpec((1, TQ, LANES), lambda b, s, i: (b, i, s)),
        out_shape=jax.ShapeDtypeStruct((nb, seq, SB_WIDTH), F32),
        scratch_shapes=[pltpu.VMEM((seq // TK, LANES, TK), BF16)],
        compiler_params=pltpu.CompilerParams(
            dimension_semantics=("arbitrary", "arbitrary", "arbitrary"),
            vmem_limit_bytes=VMEM_LIMIT_BYTES),
        name="sb_attn",
    )(q, k, v)


def _mix_out_kernel(x_ref, ylru_ref, ysb_ref, gsb_ref, wout_ref, g2_ref, wup_ref, wdn_ref, gf_ref, o_ref,
                    acc_s):
    ysb = _rms(ysb_ref[...], gsb_ref[...]).astype(BF16)
    h = x_ref[...]
    h = h + jnp.dot(ylru_ref[...], wout_ref[0:LRU_WIDTH, :], preferred_element_type=F32)
    h = h + jnp.dot(ysb, wout_ref[LRU_WIDTH:LRU_WIDTH + SB_WIDTH, :], preferred_element_type=F32)
    hn = _rms(h, g2_ref[...]).astype(BF16)
    acc_s[...] = h

    def ff_chunk(f, carry):
        up = jnp.maximum(jnp.dot(hn, wup_ref[f], preferred_element_type=F32), 0.0)
        acc_s[...] += jnp.dot((up * up).astype(BF16), wdn_ref[f], preferred_element_type=F32)
        return carry

    lax.fori_loop(0, wup_ref.shape[0], ff_chunk, 0)
    o_ref[...] = _rms(acc_s[...], gf_ref[...])


def _mix_out(x2, ylru2, ysb2, gsb, wout, g2, wup, wdn, gf):
    m = x2.shape[0]
    const2 = lambda i: (0, 0)
    const3 = lambda i: (0, 0, 0)
    row = lambda i: (i, 0)
    return pl.pallas_call(
        _mix_out_kernel,
        grid=(m // TM,),
        in_specs=[
            pl.BlockSpec((TM, D_MODEL), row),
            pl.BlockSpec((TM, LRU_WIDTH), row),
            pl.BlockSpec((TM, SB_WIDTH), row),
            pl.BlockSpec((1, SB_WIDTH), const2),
            pl.BlockSpec(wout.shape, const2),
            pl.BlockSpec((1, D_MODEL), const2),
            pl.BlockSpec(wup.shape, const3),
            pl.BlockSpec(wdn.shape, const3),
            pl.BlockSpec((1, D_MODEL), const2),
        ],
        out_specs=pl.BlockSpec((TM, D_MODEL), row),
        out_shape=jax.ShapeDtypeStruct((m, D_MODEL), F32),
        scratch_shapes=[pltpu.VMEM((TM, D_MODEL), F32)],
        compiler_params=pltpu.CompilerParams(
            dimension_semantics=("arbitrary",), vmem_limit_bytes=VMEM_LIMIT_BYTES),
        name="mix_out",
    )(x2, ylru2, ysb2, gsb, wout, g2, wup, wdn, gf)


def _block_diag(w):
    nblk, n, _ = w.shape
    eye = jnp.eye(nblk, dtype=w.dtype)
    return (eye[:, None, :, None] * w[:, :, None, :]).reshape(nblk * n, nblk * n)


def kernel(x, norm1_g, w_in, conv_w, conv_b, lru_w_a, lru_b_a, lru_w_x, lru_b_x, lru_lambda,
           lru_out_g, sb_out_g, w_out, norm2_g, w_up, w_down, final_g):
    nb, seq, d = x.shape
    assert w_in.shape[0] == 1, "kernel supports depth 1 only"
    layer = 0
    ylru, q, k, v = _mix_in(
        x, norm1_g[layer].reshape(1, d), w_in[layer].astype(BF16),
        conv_w[layer], conv_b[layer].reshape(1, LRU_WIDTH),
        _block_diag(lru_w_a[layer]).astype(BF16), lru_b_a[layer].reshape(1, LRU_WIDTH),
        _block_diag(lru_w_x[layer]).astype(BF16), lru_b_x[layer].reshape(1, LRU_WIDTH),
        lru_lambda[layer].reshape(1, LRU_WIDTH), lru_out_g[layer].reshape(1, LRU_WIDTH))
    ysb = _attn(q, k, v)
    out = _mix_out(
        x.reshape(nb * seq, d), ylru.reshape(nb * seq, LRU_WIDTH), ysb.reshape(nb * seq, SB_WIDTH),
        sb_out_g[layer].reshape(1, SB_WIDTH), w_out[layer].astype(BF16),
        norm2_g[layer].reshape(1, d),
        w_up[layer].astype(BF16).reshape(d, D_FF // TF, TF).transpose(1, 0, 2),
        w_down[layer].astype(BF16).reshape(D_FF // TF, TF, d),
        final_g.reshape(1, d))
    return out.reshape(nb, seq, d)
```

```python
import functools
import math

import jax
import jax.numpy as jnp
from jax import lax
from jax.experimental import pallas as pl
from jax.experimental.pallas import tpu as pltpu

D_MODEL = 1024
LRU_WIDTH = 512
LRU_BLOCKS = 8
LRU_BLOCK = LRU_WIDTH // LRU_BLOCKS
CONV_WIDTH = 4
LRU_C = 8.0
SB_WIDTH = 512
SB_HEADS = 8
SB_HEAD_DIM = SB_WIDTH // SB_HEADS
D_FF = 4 * D_MODEL
EPS = 1e-6

F32 = jnp.float32
BF16 = jnp.bfloat16

LANES = 128
SUBLANES = 8
VMEM_LIMIT_BYTES = 56 * 1024 * 1024

TT = 32
HALO = SUBLANES
TQ = 512
TK = 128
HEADS_PER_STEP = LANES // SB_HEAD_DIM
KEY_TILES_PER_ITER = 4
LOG2E = math.log2(math.e)
MASKED_NZ = 1e30
TM = 512
TF = 1024


def _rms(x, g):
    ms = jnp.mean(x * x, axis=-1, keepdims=True)
    return x * lax.rsqrt(ms + EPS) * g


def _sigmoid(x):
    return 1.0 / (1.0 + jnp.exp(-x))


def _gelu_tanh(x):
    c = math.sqrt(2.0 / math.pi)
    return 0.5 * x * (1.0 + jnp.tanh(c * (x + 0.044715 * (x * x * x))))


def _mix_in_kernel(x_ref, g1_ref, win_ref, cw_ref, cb_ref, wa_ref, ba_ref, wx_ref, bx_ref,
                   lam_ref, gout_ref,
                   ylru_ref, q_ref, k_ref, v_ref,
                   xl_ext, h_state, a_s, b_s, hs_s):
    nb = x_ref.shape[0]
    rows = nb * TT

    @pl.when(pl.program_id(0) == 0)
    def _():
        xl_ext[:, 0:HALO, :] = jnp.zeros((nb, HALO, LRU_WIDTH), F32)
        h_state[...] = jnp.zeros_like(h_state)

    x = x_ref[...].reshape(rows, D_MODEL)
    xn = _rms(x, g1_ref[...]).astype(BF16)

    def proj(lo, width):
        return jnp.dot(xn, win_ref[:, lo:lo + width], preferred_element_type=F32)

    xl = proj(0, LRU_WIDTH)
    gate = proj(LRU_WIDTH, LRU_WIDTH)
    base = 2 * LRU_WIDTH
    q_ref[...] = (proj(base, SB_WIDTH) * (-1.0 / math.sqrt(SB_HEAD_DIM))).astype(BF16).reshape(nb, TT, SB_WIDTH)
    k_ref[...] = proj(base + SB_WIDTH, SB_WIDTH).astype(BF16).reshape(nb, TT, SB_WIDTH)
    v_ref[...] = proj(base + 2 * SB_WIDTH, SB_WIDTH).astype(BF16).reshape(nb, TT, SB_WIDTH)

    xl_ext[:, HALO:HALO + TT, :] = xl.reshape(nb, TT, LRU_WIDTH)
    c = jnp.broadcast_to(cb_ref[...].reshape(1, 1, LRU_WIDTH), (nb, TT, LRU_WIDTH))
    for kk in range(CONV_WIDTH):
        start = HALO - (CONV_WIDTH - 1) + kk
        c = c + cw_ref[kk:kk + 1, :].reshape(1, 1, LRU_WIDTH) * xl_ext[:, start:start + TT, :]
    xl_ext[:, 0:HALO, :] = xl_ext[:, TT:TT + HALO, :]
    c = c.reshape(rows, LRU_WIDTH)

    cb16 = c.astype(BF16)
    r = _sigmoid(jnp.dot(cb16, wa_ref[...], preferred_element_type=F32) + ba_ref[...])
    i = _sigmoid(jnp.dot(cb16, wx_ref[...], preferred_element_type=F32) + bx_ref[...])
    lam = lam_ref[...]
    softplus_neg_lam = jnp.maximum(-lam, 0.0) + jnp.log(1.0 + jnp.exp(-jnp.abs(lam)))
    log_a = (-LRU_C * softplus_neg_lam) * r
    a = jnp.exp(log_a)
    bterm = jnp.sqrt(-jnp.tanh(log_a) * (a * a + 1.0)) * (i * c)
    n_groups = LRU_WIDTH // LANES
    for g in range(n_groups):
        a_s[g] = a[:, g * LANES:(g + 1) * LANES]
        b_s[g] = bterm[:, g * LANES:(g + 1) * LANES]

    for g in range(n_groups):
        h = h_state[g]
        for t in range(TT):
            h = a_s[g, pl.ds(t, nb, stride=TT), :] * h + b_s[g, pl.ds(t, nb, stride=TT), :]
            hs_s[g, pl.ds(t, nb, stride=TT), :] = h
        h_state[g] = h

    hs = jnp.concatenate([hs_s[g] for g in range(n_groups)], axis=1)
    y = hs * _gelu_tanh(gate)
    ylru_ref[...] = _rms(y, gout_ref[...]).astype(BF16).reshape(nb, TT, LRU_WIDTH)


def _mix_in(x, g1, win, cw, cb, wa, ba, wx, bx, lam, gout):
    nb, seq, _ = x.shape
    rows = nb * TT
    const2 = lambda j: (0, 0)
    out_sds = jax.ShapeDtypeStruct((nb, seq, LRU_WIDTH), BF16)
    out_spec = pl.BlockSpec((nb, TT, LRU_WIDTH), lambda j: (0, j, 0))
    return pl.pallas_call(
        _mix_in_kernel,
        grid=(seq // TT,),
        in_specs=[
            pl.BlockSpec((nb, TT, D_MODEL), lambda j: (0, j, 0)),
            pl.BlockSpec((1, D_MODEL), const2),
            pl.BlockSpec(win.shape, const2),
            pl.BlockSpec(cw.shape, const2),
            pl.BlockSpec((1, LRU_WIDTH), const2),
            pl.BlockSpec(wa.shape, const2),
            pl.BlockSpec((1, LRU_WIDTH), const2),
            pl.BlockSpec(wx.shape, const2),
            pl.BlockSpec((1, LRU_WIDTH), const2),
            pl.BlockSpec((1, LRU_WIDTH), const2),
            pl.BlockSpec((1, LRU_WIDTH), const2),
        ],
        out_specs=[out_spec, out_spec, out_spec, out_spec],
        out_shape=[out_sds, out_sds, out_sds, out_sds],
        scratch_shapes=[
            pltpu.VMEM((nb, HALO + TT, LRU_WIDTH), F32),
            pltpu.VMEM((LRU_WIDTH // LANES, nb, LANES), F32),
            pltpu.VMEM((LRU_WIDTH // LANES, rows, LANES), F32),
            pltpu.VMEM((LRU_WIDTH // LANES, rows, LANES), F32),
            pltpu.VMEM((LRU_WIDTH // LANES, rows, LANES), F32),
        ],
        compiler_params=pltpu.CompilerParams(
            dimension_semantics=("arbitrary",), vmem_limit_bytes=VMEM_LIMIT_BYTES),
        name="mix_in",
    )(x, g1, win, cw, cb, wa, ba, wx, bx, lam, gout)


def _attn_kernel(q_ref, k_ref, v_ref, o_ref, vt_s):
    qb = pl.program_id(2)
    n_kblocks = vt_s.shape[0]

    @pl.when(qb == 0)
    def _():
        for cblk in range(n_kblocks):
            blk = v_ref[0, cblk * TK:(cblk + 1) * TK, :].astype(F32)
            vt_s[cblk] = blk.T.astype(BF16)

    q = q_ref[0]
    lane = lax.broadcasted_iota(jnp.int32, (TQ, LANES), 1)
    qhs = [jnp.where((lane >= h * SB_HEAD_DIM) & (lane < (h + 1) * SB_HEAD_DIM), q, jnp.zeros_like(q))
           for h in range(HEADS_PER_STEP)]
    key_i = lax.broadcasted_iota(jnp.int32, (TK, TQ), 0)
    qry_i = lax.broadcasted_iota(jnp.int32, (TK, TQ), 1)
    upper = jnp.where(lax.broadcasted_iota(jnp.int32, (TK, TK), 1) >= lax.broadcasted_iota(jnp.int32, (TK, TK), 0),
                      1.0, 0.0).astype(BF16)

    def step(kb_first, state, masks):
        tiles = [(u, h) for u in range(KEY_TILES_PER_ITER) for h in range(HEADS_PER_STEP)]
        nz = {}
        for (u, h) in tiles:
            kblk = k_ref[0, pl.ds(pl.multiple_of((kb_first - u) * TK, TK), TK), :]
            v = lax.dot_general(kblk, qhs[h], (((1,), (1,)), ((), ())), preferred_element_type=F32)
            if masks is not None:
                v = jnp.where(masks[u], v, MASKED_NZ)
            nz[u, h] = v
        surv = {}
        for (u, h) in tiles:
            e = jnp.exp2(jnp.abs(nz[u, h]) * (-LOG2E))
            log_1m_beta = jnp.minimum(nz[u, h], 0.0) - jnp.log(1.0 + e)
            surv[u, h] = jnp.dot(upper, log_1m_beta.astype(BF16), preferred_element_type=F32)
        out = []
        for h in range(HEADS_PER_STEP):
            carry, acc = state[2 * h], state[2 * h + 1]
            atts, vts = [], []
            for u in range(KEY_TILES_PER_ITER):
                atts.append(jnp.exp((surv[u, h] - nz[u, h]) + carry).astype(BF16))
                vts.append(vt_s[kb_first - u, h * SB_HEAD_DIM:(h + 1) * SB_HEAD_DIM, :])
                carry = carry + surv[u, h][0:1, :]
            acc = acc + jnp.dot(jnp.concatenate(vts, axis=1), jnp.concatenate(atts, axis=0),
                                preferred_element_type=F32)
            out += [carry, acc]
        return tuple(out)

    n_diag = TQ // TK
    state = []
    for head in range(HEADS_PER_STEP):
        state += [jnp.zeros((1, TQ), F32), jnp.zeros((SB_HEAD_DIM, TQ), F32)]
    state = tuple(state)

    for d0 in reversed(range(0, n_diag, KEY_TILES_PER_ITER)):
        d_first = d0 + KEY_TILES_PER_ITER - 1
        masks = [(key_i + (d_first - u) * TK) < qry_i for u in range(KEY_TILES_PER_ITER)]
        state = step(qb * n_diag + d_first, state, masks)

    def body(i, state):
        return step(qb * n_diag - 1 - KEY_TILES_PER_ITER * i, state, None)

    state = lax.fori_loop(0, qb * (n_diag // KEY_TILES_PER_ITER), body, state)
    accs = [state[2 * head + 1] for head in range(HEADS_PER_STEP)]
    o_ref[0] = jnp.concatenate(accs, axis=0).T


def _attn(q, k, v):
    nb, seq, _ = q.shape
    n_slabs = SB_WIDTH // LANES
    return pl.pallas_call(
        _attn_kernel,
        grid=(nb, n_slabs, seq // TQ),
        in_specs=[
            pl.BlockSpec((1, TQ, LANES), lambda b, s, i: (b, i, s)),
            pl.BlockSpec((1, seq, LANES), lambda b, s, i: (b, 0, s)),
            pl.BlockSpec((1, seq, LANES), lambda b, s, i: (b, 0, s)),
        ],
        out_specs=pl.BlockSpec((1, TQ, LANES), lambda b, s, i: (b, i, s)),
        out_shape=jax.ShapeDtypeStruct((nb, seq, SB_WIDTH), F32),
        scratch_shapes=[pltpu.VMEM((seq // TK, LANES, TK), BF16)],
        compiler_params=pltpu.CompilerParams(
            dimension_semantics=("arbitrary", "arbitrary", "arbitrary"),
            vmem_limit_bytes=VMEM_LIMIT_BYTES),
        name="sb_attn",
    )(q, k, v)


def _mix_out_kernel(x_ref, ylru_ref, ysb_ref, gsb_ref, wout_ref, g2_ref, wup_ref, wdn_ref, gf_ref, o_ref,
                    acc_s):
    ysb = _rms(ysb_ref[...], gsb_ref[...]).astype(BF16)
    h = x_ref[...]
    h = h + jnp.dot(ylru_ref[...], wout_ref[0:LRU_WIDTH, :], preferred_element_type=F32)
    h = h + jnp.dot(ysb, wout_ref[LRU_WIDTH:LRU_WIDTH + SB_WIDTH, :], preferred_element_type=F32)
    hn = _rms(h, g2_ref[...]).astype(BF16)
    acc_s[...] = h

    def ff_chunk(f, carry):
        up = jnp.maximum(jnp.dot(hn, wup_ref[f], preferred_element_type=F32), 0.0)
        acc_s[...] += jnp.dot((up * up).astype(BF16), wdn_ref[f], preferred_element_type=F32)
        return carry

    lax.fori_loop(0, wup_ref.shape[0], ff_chunk, 0)
    o_ref[...] = _rms(acc_s[...], gf_ref[...])


def _mix_out(x2, ylru2, ysb2, gsb, wout, g2, wup, wdn, gf):
    m = x2.shape[0]
    const2 = lambda i: (0, 0)
    const3 = lambda i: (0, 0, 0)
    row = lambda i: (i, 0)
    return pl.pallas_call(
        _mix_out_kernel,
        grid=(m // TM,),
        in_specs=[
            pl.BlockSpec((TM, D_MODEL), row),
            pl.BlockSpec((TM, LRU_WIDTH), row),
            pl.BlockSpec((TM, SB_WIDTH), row),
            pl.BlockSpec((1, SB_WIDTH), const2),
            pl.BlockSpec(wout.shape, const2),
            pl.BlockSpec((1, D_MODEL), const2),
            pl.BlockSpec(wup.shape, const3),
            pl.BlockSpec(wdn.shape, const3),
            pl.BlockSpec((1, D_MODEL), const2),
        ],
        out_specs=pl.BlockSpec((TM, D_MODEL), row),
        out_shape=jax.ShapeDtypeStruct((m, D_MODEL), F32),
        scratch_shapes=[pltpu.VMEM((TM, D_MODEL), F32)],
        compiler_params=pltpu.CompilerParams(
            dimension_semantics=("arbitrary",), vmem_limit_bytes=VMEM_LIMIT_BYTES),
        name="mix_out",
    )(x2, ylru2, ysb2, gsb, wout, g2, wup, wdn, gf)


def _block_diag(w):
    nblk, n, _ = w.shape
    eye = jnp.eye(nblk, dtype=w.dtype)
    return (eye[:, None, :, None] * w[:, :, None, :]).reshape(nblk * n, nblk * n)


def kernel(x, norm1_g, w_in, conv_w, conv_b, lru_w_a, lru_b_a, lru_w_x, lru_b_x, lru_lambda,
           lru_out_g, sb_out_g, w_out, norm2_g, w_up, w_down, final_g):
    nb, seq, d = x.shape
    assert w_in.shape[0] == 1, "kernel supports depth 1 only"
    layer = 0
    ylru, q, k, v = _mix_in(
        x, norm1_g[layer].reshape(1, d), w_in[layer].astype(BF16),
        conv_w[layer], conv_b[layer].reshape(1, LRU_WIDTH),
        _block_diag(lru_w_a[layer]).astype(BF16), lru_b_a[layer].reshape(1, LRU_WIDTH),
        _block_diag(lru_w_x[layer]).astype(BF16), lru_b_x[layer].reshape(1, LRU_WIDTH),
        lru_lambda[layer].reshape(1, LRU_WIDTH), lru_out_g[layer].reshape(1, LRU_WIDTH))
    ysb = _attn(q, k, v)
    out = _mix_out(
        x.reshape(nb * seq, d), ylru.reshape(nb * seq, LRU_WIDTH), ysb.reshape(nb * seq, SB_WIDTH),
        sb_out_g[layer].reshape(1, SB_WIDTH), w_out[layer].astype(BF16),
        norm2_g[layer].reshape(1, d),
        w_up[layer].astype(BF16).reshape(d, D_FF // TF, TF).transpose(1, 0, 2),
        w_down[layer].astype(BF16).reshape(D_FF // TF, TF, d),
        final_g.reshape(1, d))
    return out.reshape(nb, seq, d)
```

```python
import math

import numpy as np

import jax
import jax.numpy as jnp
from jax import lax
from jax.experimental import pallas as pl
from jax.experimental.pallas import tpu as pltpu

D_MODEL = 1024
LRU_WIDTH = 512
LRU_BLOCKS = 8
LRU_BLOCK = LRU_WIDTH // LRU_BLOCKS
CONV_WIDTH = 4
LRU_C = 8.0
SB_WIDTH = 512
SB_HEADS = 8
SB_HEAD_DIM = SB_WIDTH // SB_HEADS
D_FF = 4 * D_MODEL
EPS = 1e-6

F32 = jnp.float32
BF16 = jnp.bfloat16

LANES = 128
SUBLANES = 8
VMEM_LIMIT_BYTES = 56 * 1024 * 1024

TT = 32
TQ = 512
TK = 128
HEADS_PER_STEP = LANES // SB_HEAD_DIM
KEY_TILES_PER_ITER = 2
LOG2E = math.log2(math.e)
MASKED_NZ = 1e30
F32_EXP_UNDERFLOW = -105.0
TM = 512
TF = 1024


def _rms(x, g):
    ms = jnp.mean(x * x, axis=-1, keepdims=True)
    return x * lax.rsqrt(ms + EPS) * g


def _sigmoid(x):
    return 1.0 / (1.0 + jnp.exp(-x))


def _gelu_tanh(x):
    c = math.sqrt(2.0 / math.pi)
    return 0.5 * x * (1.0 + jnp.tanh(c * (x + 0.044715 * (x * x * x))))


def _mix_in_kernel(x_ref, perm_ref, permt_ref, g1_ref, win_ref, cw_ref, cb_ref, wa_ref, ba_ref, wx_ref, bx_ref,
                   lam_ref, gout_ref,
                   ylru_ref, q_ref, k_ref, v_ref,
                   xl_ext, h_state):
    nb = x_ref.shape[0]
    rows = nb * TT
    halo = (CONV_WIDTH - 1) * nb

    @pl.when(pl.program_id(0) == 0)
    def _():
        xl_ext[0:halo, :] = jnp.zeros((halo, LRU_WIDTH), F32)
        h_state[...] = jnp.zeros_like(h_state)

    x = x_ref[...].reshape(rows, D_MODEL)
    xn = _rms(x, g1_ref[...]).astype(BF16)

    def proj(lhs, lo, width):
        return jnp.dot(lhs, win_ref[:, lo:lo + width], preferred_element_type=F32)

    base = 2 * LRU_WIDTH

    xt = jnp.dot(perm_ref[...], xn, preferred_element_type=F32).astype(BF16)
    xl = proj(xt, 0, LRU_WIDTH)
    gate = proj(xt, LRU_WIDTH, LRU_WIDTH)

    q_ref[...] = (proj(xn, base, SB_WIDTH) * (-1.0 / math.sqrt(SB_HEAD_DIM))).astype(BF16).reshape(nb, TT, SB_WIDTH)

    xl_ext[halo:halo + rows, :] = xl
    c = cb_ref[...] + cw_ref[0:1, :] * xl_ext[0:rows, :]
    for kk in range(1, CONV_WIDTH):
        c = c + cw_ref[kk:kk + 1, :] * xl_ext[kk * nb:kk * nb + rows, :]
    xl_ext[0:halo, :] = xl_ext[rows:rows + halo, :]

    cb16 = c.astype(BF16)
    r = _sigmoid(jnp.dot(cb16, wa_ref[...], preferred_element_type=F32) + ba_ref[...])
    i = _sigmoid(jnp.dot(cb16, wx_ref[...], preferred_element_type=F32) + bx_ref[...])

    k_ref[...] = proj(xn, base + SB_WIDTH, SB_WIDTH).astype(BF16).reshape(nb, TT, SB_WIDTH)
    v_ref[...] = proj(xn, base + 2 * SB_WIDTH, SB_WIDTH).astype(BF16).reshape(nb, TT, SB_WIDTH)
    lam = lam_ref[...]
    softplus_neg_lam = jnp.maximum(-lam, 0.0) + jnp.log(1.0 + jnp.exp(-jnp.abs(lam)))
    log_a = (-LRU_C * softplus_neg_lam) * r
    a = jnp.exp(log_a)
    bterm = jnp.sqrt(-jnp.tanh(log_a) * (a * a + 1.0)) * (i * c)

    h = h_state[...]
    hs = []
    for t in range(TT):
        h = a[t * nb:(t + 1) * nb, :] * h + bterm[t * nb:(t + 1) * nb, :]
        hs.append(h)
    h_state[...] = h

    y = jnp.concatenate(hs, axis=0) * _gelu_tanh(gate)
    yn = _rms(y, gout_ref[...]).astype(BF16)
    ylru_ref[...] = jnp.dot(permt_ref[...], yn, preferred_element_type=F32).astype(BF16).reshape(nb, TT, LRU_WIDTH)


def _time_major_perm(nb):
    rows = nb * TT
    p = np.zeros((rows, rows), np.float32)
    for b in range(nb):
        for t in range(TT):
            p[t * nb + b, b * TT + t] = 1.0
    return p


def _mix_in(x, g1, win, cw, cb, wa, ba, wx, bx, lam, gout):
    nb, seq, _ = x.shape
    rows = nb * TT
    perm = _time_major_perm(nb)
    const2 = lambda j: (0, 0)
    out_sds = jax.ShapeDtypeStruct((nb, seq, LRU_WIDTH), BF16)
    out_spec = pl.BlockSpec((nb, TT, LRU_WIDTH), lambda j: (0, j, 0))
    return pl.pallas_call(
        _mix_in_kernel,
        grid=(seq // TT,),
        in_specs=[
            pl.BlockSpec((nb, TT, D_MODEL), lambda j: (0, j, 0)),
            pl.BlockSpec((rows, rows), const2),
            pl.BlockSpec((rows, rows), const2),
            pl.BlockSpec((1, D_MODEL), const2),
            pl.BlockSpec(win.shape, const2),
            pl.BlockSpec(cw.shape, const2),
            pl.BlockSpec((1, LRU_WIDTH), const2),
            pl.BlockSpec(wa.shape, const2),
            pl.BlockSpec((1, LRU_WIDTH), const2),
            pl.BlockSpec(wx.shape, const2),
            pl.BlockSpec((1, LRU_WIDTH), const2),
            pl.BlockSpec((1, LRU_WIDTH), const2),
            pl.BlockSpec((1, LRU_WIDTH), const2),
        ],
        out_specs=[out_spec, out_spec, out_spec, out_spec],
        out_shape=[out_sds, out_sds, out_sds, out_sds],
        scratch_shapes=[
            pltpu.VMEM(((CONV_WIDTH - 1) * nb + rows, LRU_WIDTH), F32),
            pltpu.VMEM((nb, LRU_WIDTH), F32),
        ],
        compiler_params=pltpu.CompilerParams(
            dimension_semantics=("arbitrary",), vmem_limit_bytes=VMEM_LIMIT_BYTES),
        name="mix_in",
    )(x, jnp.asarray(perm, BF16), jnp.asarray(perm.T, BF16), g1, win, cw, cb, wa, ba, wx, bx, lam, gout)


def _attn_kernel(q_ref, k_ref, v_ref, o_ref, vt_s):
    qb = pl.program_id(2)
    n_kblocks = vt_s.shape[0]

    @pl.when(qb == 0)
    def _():
        for cblk in range(n_kblocks):
            blk = v_ref[0, cblk * TK:(cblk + 1) * TK, :].astype(F32)
            vt_s[cblk] = blk.T.astype(BF16)

    q = q_ref[0]
    lane = lax.broadcasted_iota(jnp.int32, (TQ, LANES), 1)
    qhs = [jnp.where((lane >= h * SB_HEAD_DIM) & (lane < (h + 1) * SB_HEAD_DIM), q, jnp.zeros_like(q))
           for h in range(HEADS_PER_STEP)]
    key_i = lax.broadcasted_iota(jnp.int32, (TK, TQ), 0)
    qry_i = lax.broadcasted_iota(jnp.int32, (TK, TQ), 1)
    upper = jnp.where(lax.broadcasted_iota(jnp.int32, (TK, TK), 1) >= lax.broadcasted_iota(jnp.int32, (TK, TK), 0),
                      1.0, 0.0).astype(BF16)

    def step(kb_first, state, masks):
        tiles = [(u, h) for u in range(KEY_TILES_PER_ITER) for h in range(HEADS_PER_STEP)]
        nz = {}
        for (u, h) in tiles:
            kblk = k_ref[0, pl.ds(pl.multiple_of((kb_first - u) * TK, TK), TK), :]
            v = lax.dot_general(kblk, qhs[h], (((1,), (1,)), ((), ())), preferred_element_type=F32)
            if masks is not None:
                v = jnp.where(masks[u], v, MASKED_NZ)
            nz[u, h] = v
        surv = {}
        for (u, h) in tiles:
            e = jnp.exp2(jnp.abs(nz[u, h]) * (-LOG2E))
            log_1m_beta = jnp.minimum(nz[u, h], 0.0) - jnp.log(1.0 + e)
            surv[u, h] = jnp.dot(upper, log_1m_beta.astype(BF16), preferred_element_type=F32)
        out = []
        for h in range(HEADS_PER_STEP):
            carry, acc = state[2 * h], state[2 * h + 1]
            atts, vts = [], []
            for u in range(KEY_TILES_PER_ITER):
                atts.append(jnp.exp((surv[u, h] - nz[u, h]) + carry).astype(BF16))
                vts.append(vt_s[kb_first - u, h * SB_HEAD_DIM:(h + 1) * SB_HEAD_DIM, :])
                carry = carry + surv[u, h][0:1, :]
            acc = acc + jnp.dot(jnp.concatenate(vts, axis=1), jnp.concatenate(atts, axis=0),
                                preferred_element_type=F32)
            out += [carry, acc]
        return tuple(out)

    n_diag = TQ // TK
    state = []
    for head in range(HEADS_PER_STEP):
        state += [jnp.zeros((1, TQ), F32), jnp.zeros((SB_HEAD_DIM, TQ), F32)]
    state = tuple(state)

    for d0 in reversed(range(0, n_diag, KEY_TILES_PER_ITER)):
        d_first = d0 + KEY_TILES_PER_ITER - 1
        masks = [(key_i + (d_first - u) * TK) < qry_i for u in range(KEY_TILES_PER_ITER)]
        state = step(qb * n_diag + d_first, state, masks)

    n_steps = qb * (n_diag // KEY_TILES_PER_ITER)

    def live(loop_state):
        i, state = loop_state[0], loop_state[1:]
        top = state[0]
        for head in range(1, HEADS_PER_STEP):
            top = jnp.maximum(top, state[2 * head])
        return jnp.logical_and(i < n_steps, jnp.max(top) >= F32_EXP_UNDERFLOW)

    def body(loop_state):
        i, state = loop_state[0], loop_state[1:]
        return (i + 1,) + step(qb * n_diag - 1 - KEY_TILES_PER_ITER * i, state, None)

    state = lax.while_loop(live, body, (jnp.int32(0),) + state)[1:]
    accs = [state[2 * head + 1] for head in range(HEADS_PER_STEP)]
    o_ref[0] = jnp.concatenate(accs, axis=0).T


def _attn(q, k, v):
    nb, seq, _ = q.shape
    n_slabs = SB_WIDTH // LANES
    return pl.pallas_call(
        _attn_kernel,
        grid=(nb, n_slabs, seq // TQ),
        in_specs=[
            pl.BlockSpec((1, TQ, LANES), lambda b, s, i: (b, i, s)),
            pl.BlockSpec((1, seq, LANES), lambda b, s, i: (b, 0, s)),
            pl.BlockSpec((1, seq, LANES), lambda b, s, i: (b, 0, s)),
        ],
        out_specs=pl.BlockSpec((1, TQ, LANES), lambda b, s, i: (b, i, s)),
        out_shape=jax.ShapeDtypeStruct((nb, seq, SB_WIDTH), F32),
        scratch_shapes=[pltpu.VMEM((seq // TK, LANES, TK), BF16)],
        compiler_params=pltpu.CompilerParams(
            dimension_semantics=("arbitrary", "arbitrary", "arbitrary"),
            vmem_limit_bytes=VMEM_LIMIT_BYTES),
        name="sb_attn",
    )(q, k, v)


def _mix_out_kernel(x_ref, ylru_ref, ysb_ref, gsb_ref, wout_ref, g2_ref, wup_ref, wdn_ref, gf_ref, o_ref,
                    acc_s):
    ysb = _rms(ysb_ref[...], gsb_ref[...]).astype(BF16)
    h = x_ref[...]
    h = h + jnp.dot(ylru_ref[...], wout_ref[0:LRU_WIDTH, :], preferred_element_type=F32)
    h = h + jnp.dot(ysb, wout_ref[LRU_WIDTH:LRU_WIDTH + SB_WIDTH, :], preferred_element_type=F32)
    hn = _rms(h, g2_ref[...]).astype(BF16)
    acc_s[...] = h

    def ff_chunk(f, carry):
        up = jnp.maximum(jnp.dot(hn, wup_ref[f], preferred_element_type=F32), 0.0)
        acc_s[...] += jnp.dot((up * up).astype(BF16), wdn_ref[f], preferred_element_type=F32)
        return carry

    lax.fori_loop(0, wup_ref.shape[0], ff_chunk, 0)
    o_ref[...] = _rms(acc_s[...], gf_ref[...])


def _mix_out(x2, ylru2, ysb2, gsb, wout, g2, wup, wdn, gf):
    m = x2.shape[0]
    const2 = lambda i: (0, 0)
    const3 = lambda i: (0, 0, 0)
    row = lambda i: (i, 0)
    return pl.pallas_call(
        _mix_out_kernel,
        grid=(m // TM,),
        in_specs=[
            pl.BlockSpec((TM, D_MODEL), row),
            pl.BlockSpec((TM, LRU_WIDTH), row),
            pl.BlockSpec((TM, SB_WIDTH), row),
            pl.BlockSpec((1, SB_WIDTH), const2),
            pl.BlockSpec(wout.shape, const2),
            pl.BlockSpec((1, D_MODEL), const2),
            pl.BlockSpec(wup.shape, const3),
            pl.BlockSpec(wdn.shape, const3),
            pl.BlockSpec((1, D_MODEL), const2),
        ],
        out_specs=pl.BlockSpec((TM, D_MODEL), row),
        out_shape=jax.ShapeDtypeStruct((m, D_MODEL), F32),
        scratch_shapes=[pltpu.VMEM((TM, D_MODEL), F32)],
        compiler_params=pltpu.CompilerParams(
            dimension_semantics=("arbitrary",), vmem_limit_bytes=VMEM_LIMIT_BYTES),
        name="mix_out",
    )(x2, ylru2, ysb2, gsb, wout, g2, wup, wdn, gf)


def _block_diag(w):
    nblk, n, _ = w.shape
    eye = jnp.eye(nblk, dtype=w.dtype)
    return (eye[:, None, :, None] * w[:, :, None, :]).reshape(nblk * n, nblk * n)


def kernel(x, norm1_g, w_in, conv_w, conv_b, lru_w_a, lru_b_a, lru_w_x, lru_b_x, lru_lambda,
           lru_out_g, sb_out_g, w_out, norm2_g, w_up, w_down, final_g):
    nb, seq, d = x.shape
    assert w_in.shape[0] == 1, "kernel supports depth 1 only"
    layer = 0
    ylru, q, k, v = _mix_in(
        x, norm1_g[layer].reshape(1, d), w_in[layer].astype(BF16),
        conv_w[layer], conv_b[layer].reshape(1, LRU_WIDTH),
        _block_diag(lru_w_a[layer]).astype(BF16), lru_b_a[layer].reshape(1, LRU_WIDTH),
        _block_diag(lru_w_x[layer]).astype(BF16), lru_b_x[layer].reshape(1, LRU_WIDTH),
        lru_lambda[layer].reshape(1, LRU_WIDTH), lru_out_g[layer].reshape(1, LRU_WIDTH))
    ysb = _attn(q, k, v)
    out = _mix_out(
        x.reshape(nb * seq, d), ylru.reshape(nb * seq, LRU_WIDTH), ysb.reshape(nb * seq, SB_WIDTH),
        sb_out_g[layer].reshape(1, SB_WIDTH), w_out[layer].astype(BF16),
        norm2_g[layer].reshape(1, d),
        w_up[layer].astype(BF16).reshape(d, D_FF // TF, TF).transpose(1, 0, 2),
        w_down[layer].astype(BF16).reshape(D_FF // TF, TF, d),
        final_g.reshape(1, d))
    return out.reshape(nb, seq, d)
```

```python
import math

import numpy as np

import jax
import jax.numpy as jnp
from jax import lax
from jax.experimental import pallas as pl
from jax.experimental.pallas import tpu as pltpu

D_MODEL = 1024
LRU_WIDTH = 512
LRU_BLOCKS = 8
LRU_BLOCK = LRU_WIDTH // LRU_BLOCKS
CONV_WIDTH = 4
LRU_C = 8.0
SB_WIDTH = 512
SB_HEADS = 8
SB_HEAD_DIM = SB_WIDTH // SB_HEADS
D_FF = 4 * D_MODEL
EPS = 1e-6

F32 = jnp.float32
BF16 = jnp.bfloat16

LANES = 128
SUBLANES = 8
VMEM_LIMIT_BYTES = 56 * 1024 * 1024

TT = 32
TQ = 512
TK = 128
HEADS_PER_STEP = LANES // SB_HEAD_DIM
KEY_TILES_PER_ITER = 2
LOG2E = math.log2(math.e)
MASKED_NZ = 1e30
F32_EXP_UNDERFLOW = -105.0
TM = 512
TF = 1024


def _rms(x, g):
    ms = jnp.mean(x * x, axis=-1, keepdims=True)
    return x * lax.rsqrt(ms + EPS) * g


def _sigmoid(x):
    return 1.0 / (1.0 + jnp.exp(-x))


def _gelu_tanh(x):
    c = math.sqrt(2.0 / math.pi)
    return 0.5 * x * (1.0 + jnp.tanh(c * (x + 0.044715 * (x * x * x))))


def _mix_in_kernel(x_ref, perm_ref, permt_ref, g1_ref, win_ref, cw_ref, cb_ref, wa_ref, ba_ref, wx_ref, bx_ref,
                   lam_ref, gout_ref,
                   ylru_ref, q_ref, k_ref, v_ref,
                   xl_ext, h_state):
    nb = x_ref.shape[0]
    rows = nb * TT
    halo = (CONV_WIDTH - 1) * nb

    @pl.when(pl.program_id(0) == 0)
    def _():
        xl_ext[0:halo, :] = jnp.zeros((halo, LRU_WIDTH), F32)
        h_state[...] = jnp.zeros_like(h_state)

    x = x_ref[...].reshape(rows, D_MODEL)
    xn = _rms(x, g1_ref[...]).astype(BF16)

    def proj(lhs, lo, width):
        return jnp.dot(lhs, win_ref[:, lo:lo + width], preferred_element_type=F32)

    base = 2 * LRU_WIDTH

    xt = jnp.dot(perm_ref[...], xn, preferred_element_type=F32).astype(BF16)
    xl = proj(xt, 0, LRU_WIDTH)
    gate = proj(xt, LRU_WIDTH, LRU_WIDTH)

    q_ref[...] = (proj(xn, base, SB_WIDTH) * (-1.0 / math.sqrt(SB_HEAD_DIM))).astype(BF16).reshape(nb, TT, SB_WIDTH)

    xl_ext[halo:halo + rows, :] = xl
    c = cb_ref[...] + cw_ref[0:1, :] * xl_ext[0:rows, :]
    for kk in range(1, CONV_WIDTH):
        c = c + cw_ref[kk:kk + 1, :] * xl_ext[kk * nb:kk * nb + rows, :]
    xl_ext[0:halo, :] = xl_ext[rows:rows + halo, :]

    cb16 = c.astype(BF16)
    r = _sigmoid(jnp.dot(cb16, wa_ref[...], preferred_element_type=F32) + ba_ref[...])
    i = _sigmoid(jnp.dot(cb16, wx_ref[...], preferred_element_type=F32) + bx_ref[...])

    k_ref[...] = proj(xn, base + SB_WIDTH, SB_WIDTH).astype(BF16).reshape(nb, TT, SB_WIDTH)
    v_ref[...] = proj(xn, base + 2 * SB_WIDTH, SB_WIDTH).astype(BF16).reshape(nb, TT, SB_WIDTH)
    lam = lam_ref[...]
    softplus_neg_lam = jnp.maximum(-lam, 0.0) + jnp.log(1.0 + jnp.exp(-jnp.abs(lam)))
    log_a = (-LRU_C * softplus_neg_lam) * r
    a = jnp.exp(log_a)
    bterm = jnp.sqrt(-jnp.tanh(log_a) * (a * a + 1.0)) * (i * c)

    h = h_state[...]
    hs = []
    for t in range(TT):
        h = a[t * nb:(t + 1) * nb, :] * h + bterm[t * nb:(t + 1) * nb, :]
        hs.append(h)
    h_state[...] = h

    y = jnp.concatenate(hs, axis=0) * _gelu_tanh(gate)
    yn = _rms(y, gout_ref[...]).astype(BF16)
    ylru_ref[...] = jnp.dot(permt_ref[...], yn, preferred_element_type=F32).astype(BF16).reshape(nb, TT, LRU_WIDTH)


def _time_major_perm(nb):
    rows = nb * TT
    p = np.zeros((rows, rows), np.float32)
    for b in range(nb):
        for t in range(TT):
            p[t * nb + b, b * TT + t] = 1.0
    return p


def _mix_in(x, g1, win, cw, cb, wa, ba, wx, bx, lam, gout):
    nb, seq, _ = x.shape
    rows = nb * TT
    perm = _time_major_perm(nb)
    const2 = lambda j: (0, 0)
    out_sds = jax.ShapeDtypeStruct((nb, seq, LRU_WIDTH), BF16)
    out_spec = pl.BlockSpec((nb, TT, LRU_WIDTH), lambda j: (0, j, 0))
    return pl.pallas_call(
        _mix_in_kernel,
        grid=(seq // TT,),
        in_specs=[
            pl.BlockSpec((nb, TT, D_MODEL), lambda j: (0, j, 0)),
            pl.BlockSpec((rows, rows), const2),
            pl.BlockSpec((rows, rows), const2),
            pl.BlockSpec((1, D_MODEL), const2),
            pl.BlockSpec(win.shape, const2),
            pl.BlockSpec(cw.shape, const2),
            pl.BlockSpec((1, LRU_WIDTH), const2),
            pl.BlockSpec(wa.shape, const2),
            pl.BlockSpec((1, LRU_WIDTH), const2),
            pl.BlockSpec(wx.shape, const2),
            pl.BlockSpec((1, LRU_WIDTH), const2),
            pl.BlockSpec((1, LRU_WIDTH), const2),
            pl.BlockSpec((1, LRU_WIDTH), const2),
        ],
        out_specs=[out_spec, out_spec, out_spec, out_spec],
        out_shape=[out_sds, out_sds, out_sds, out_sds],
        scratch_shapes=[
            pltpu.VMEM(((CONV_WIDTH - 1) * nb + rows, LRU_WIDTH), F32),
            pltpu.VMEM((nb, LRU_WIDTH), F32),
        ],
        compiler_params=pltpu.CompilerParams(
            dimension_semantics=("arbitrary",), vmem_limit_bytes=VMEM_LIMIT_BYTES),
        name="mix_in",
    )(x, jnp.asarray(perm, BF16), jnp.asarray(perm.T, BF16), g1, win, cw, cb, wa, ba, wx, bx, lam, gout)


def _attn_kernel(q_ref, k_ref, v_ref, o_ref, vt_s):
    qb = pl.program_id(2)
    n_kblocks = vt_s.shape[0]

    @pl.when(qb == 0)
    def _():
        for cblk in range(n_kblocks):
            blk = v_ref[0, cblk * TK:(cblk + 1) * TK, :].astype(F32)
            vt_s[cblk] = blk.T.astype(BF16)

    q = q_ref[0]
    lane = lax.broadcasted_iota(jnp.int32, (TQ, LANES), 1)
    qhs = [jnp.where((lane >= h * SB_HEAD_DIM) & (lane < (h + 1) * SB_HEAD_DIM), q, jnp.zeros_like(q))
           for h in range(HEADS_PER_STEP)]
    row_i = lax.broadcasted_iota(jnp.int32, (TK, TK), 0)
    col_i = lax.broadcasted_iota(jnp.int32, (TK, TK), 1)
    key_before_query = row_i < col_i
    upper = jnp.where(col_i >= row_i, 1.0, 0.0).astype(BF16)

    def run(tiles, state):
        work = [(n, h) for n in range(len(tiles)) for h in range(HEADS_PER_STEP)]
        nz = {}
        for (n, h) in work:
            kb, col0, diagonal, _ = tiles[n]
            kblk = k_ref[0, pl.ds(pl.multiple_of(kb * TK, TK), TK), :]
            v = lax.dot_general(kblk, qhs[h][col0:, :], (((1,), (1,)), ((), ())),
                                preferred_element_type=F32)
            if diagonal:
                first = jnp.where(key_before_query, v[:, :TK], MASKED_NZ)
                v = first if v.shape[1] == TK else jnp.concatenate([first, v[:, TK:]], axis=1)
            nz[n, h] = v
        surv = {}
        for (n, h) in work:
            e = jnp.exp2(jnp.abs(nz[n, h]) * (-LOG2E))
            log_1m_beta = jnp.minimum(nz[n, h], 0.0) - jnp.log(1.0 + e)
            surv[n, h] = jnp.dot(upper, log_1m_beta.astype(BF16), preferred_element_type=F32)
        out = []
        for h in range(HEADS_PER_STEP):
            carry, acc = state[2 * h], state[2 * h + 1]
            atts, vts = [], []
            for n, (kb, col0, _, bias) in enumerate(tiles):
                seen = carry[:, col0:]
                seen_b = seen if bias is None else seen + bias
                att = jnp.exp((surv[n, h] - nz[n, h]) + seen_b).astype(BF16)
                vt = vt_s[kb, h * SB_HEAD_DIM:(h + 1) * SB_HEAD_DIM, :]
                if col0 == 0:
                    atts.append(att)
                    vts.append(vt)
                    carry = carry + surv[n, h][0:1, :]
                else:
                    part = acc[:, col0:] + jnp.dot(vt, att, preferred_element_type=F32)
                    acc = jnp.concatenate([acc[:, :col0], part], axis=1)
                    carry = jnp.concatenate([carry[:, :col0], seen + surv[n, h][0:1, :]], axis=1)
            acc = acc + jnp.dot(jnp.concatenate(vts, axis=1), jnp.concatenate(atts, axis=0),
                                preferred_element_type=F32)
            out += [carry, acc]
        return tuple(out)

    def some_query_alive(state):
        top = state[0]
        for head in range(1, HEADS_PER_STEP):
            top = jnp.maximum(top, state[2 * head])
        return jnp.max(top) >= F32_EXP_UNDERFLOW

    n_diag = TQ // TK
    state = []
    for head in range(HEADS_PER_STEP):
        state += [jnp.zeros((1, TQ), F32), jnp.zeros((SB_HEAD_DIM, TQ), F32)]
    no_earlier = jnp.where(qb > 0, 0.0, -MASKED_NZ).astype(F32)
    tiles = [(qb * n_diag + d, d * TK, True, None) for d in reversed(range(n_diag))]
    tiles += [(jnp.maximum(qb * n_diag - 1 - u, 0), 0, False, no_earlier) for u in range(KEY_TILES_PER_ITER)]
    state = run(tiles, tuple(state))

    n_steps = qb * (n_diag // KEY_TILES_PER_ITER)

    def body(loop_state):
        i, state = loop_state[0], loop_state[2:]
        kb = qb * n_diag - 1 - KEY_TILES_PER_ITER * i
        state = run([(kb - u, 0, False, None) for u in range(KEY_TILES_PER_ITER)], state)
        return (i + 1, jnp.logical_and(i + 1 < n_steps, some_query_alive(state))) + state

    go = jnp.logical_and(1 < n_steps, some_query_alive(state))
    state = lax.while_loop(lambda loop_state: loop_state[1], body, (jnp.int32(1), go) + state)[2:]
    accs = [state[2 * head + 1] for head in range(HEADS_PER_STEP)]
    o_ref[0] = jnp.concatenate(accs, axis=0).T


def _attn(q, k, v):
    nb, seq, _ = q.shape
    n_slabs = SB_WIDTH // LANES
    return pl.pallas_call(
        _attn_kernel,
        grid=(nb, n_slabs, seq // TQ),
        in_specs=[
            pl.BlockSpec((1, TQ, LANES), lambda b, s, i: (b, i, s)),
            pl.BlockSpec((1, seq, LANES), lambda b, s, i: (b, 0, s)),
            pl.BlockSpec((1, seq, LANES), lambda b, s, i: (b, 0, s)),
        ],
        out_specs=pl.BlockSpec((1, TQ, LANES), lambda b, s, i: (b, i, s)),
        out_shape=jax.ShapeDtypeStruct((nb, seq, SB_WIDTH), F32),
        scratch_shapes=[pltpu.VMEM((seq // TK, LANES, TK), BF16)],
        compiler_params=pltpu.CompilerParams(
            dimension_semantics=("arbitrary", "arbitrary", "arbitrary"),
            vmem_limit_bytes=VMEM_LIMIT_BYTES),
        name="sb_attn",
    )(q, k, v)


def _mix_out_kernel(x_ref, ylru_ref, ysb_ref, gsb_ref, wout_ref, g2_ref, wup_ref, wdn_ref, gf_ref, o_ref,
                    acc_s):
    ysb = _rms(ysb_ref[...], gsb_ref[...]).astype(BF16)
    h = x_ref[...]
    h = h + jnp.dot(ylru_ref[...], wout_ref[0:LRU_WIDTH, :], preferred_element_type=F32)
    h = h + jnp.dot(ysb, wout_ref[LRU_WIDTH:LRU_WIDTH + SB_WIDTH, :], preferred_element_type=F32)
    hn = _rms(h, g2_ref[...]).astype(BF16)
    acc_s[...] = h

    def ff_chunk(f, carry):
        up = jnp.maximum(jnp.dot(hn, wup_ref[f], preferred_element_type=F32), 0.0)
        acc_s[...] += jnp.dot((up * up).astype(BF16), wdn_ref[f], preferred_element_type=F32)
        return carry

    lax.fori_loop(0, wup_ref.shape[0], ff_chunk, 0)
    o_ref[...] = _rms(acc_s[...], gf_ref[...])


def _mix_out(x2, ylru2, ysb2, gsb, wout, g2, wup, wdn, gf):
    m = x2.shape[0]
    const2 = lambda i: (0, 0)
    const3 = lambda i: (0, 0, 0)
    row = lambda i: (i, 0)
    return pl.pallas_call(
        _mix_out_kernel,
        grid=(m // TM,),
        in_specs=[
            pl.BlockSpec((TM, D_MODEL), row),
            pl.BlockSpec((TM, LRU_WIDTH), row),
            pl.BlockSpec((TM, SB_WIDTH), row),
            pl.BlockSpec((1, SB_WIDTH), const2),
            pl.BlockSpec(wout.shape, const2),
            pl.BlockSpec((1, D_MODEL), const2),
            pl.BlockSpec(wup.shape, const3),
            pl.BlockSpec(wdn.shape, const3),
            pl.BlockSpec((1, D_MODEL), const2),
        ],
        out_specs=pl.BlockSpec((TM, D_MODEL), row),
        out_shape=jax.ShapeDtypeStruct((m, D_MODEL), F32),
        scratch_shapes=[pltpu.VMEM((TM, D_MODEL), F32)],
        compiler_params=pltpu.CompilerParams(
            dimension_semantics=("arbitrary",), vmem_limit_bytes=VMEM_LIMIT_BYTES),
        name="mix_out",
    )(x2, ylru2, ysb2, gsb, wout, g2, wup, wdn, gf)


def _block_diag(w):
    nblk, n, _ = w.shape
    eye = jnp.eye(nblk, dtype=w.dtype)
    return (eye[:, None, :, None] * w[:, :, None, :]).reshape(nblk * n, nblk * n)


def kernel(x, norm1_g, w_in, conv_w, conv_b, lru_w_a, lru_b_a, lru_w_x, lru_b_x, lru_lambda,
           lru_out_g, sb_out_g, w_out, norm2_g, w_up, w_down, final_g):
    nb, seq, d = x.shape
    assert w_in.shape[0] == 1, "kernel supports depth 1 only"
    layer = 0
    ylru, q, k, v = _mix_in(
        x, norm1_g[layer].reshape(1, d), w_in[layer].astype(BF16),
        conv_w[layer], conv_b[layer].reshape(1, LRU_WIDTH),
        _block_diag(lru_w_a[layer]).astype(BF16), lru_b_a[layer].reshape(1, LRU_WIDTH),
        _block_diag(lru_w_x[layer]).astype(BF16), lru_b_x[layer].reshape(1, LRU_WIDTH),
        lru_lambda[layer].reshape(1, LRU_WIDTH), lru_out_g[layer].reshape(1, LRU_WIDTH))
    ysb = _attn(q, k, v)
    out = _mix_out(
        x.reshape(nb * seq, d), ylru.reshape(nb * seq, LRU_WIDTH), ysb.reshape(nb * seq, SB_WIDTH),
        sb_out_g[layer].reshape(1, SB_WIDTH), w_out[layer].astype(BF16),
        norm2_g[layer].reshape(1, d),
        w_up[layer].astype(BF16).reshape(d, D_FF // TF, TF).transpose(1, 0, 2),
        w_down[layer].astype(BF16).reshape(D_FF // TF, TF, d),
        final_g.reshape(1, d))
    return out.reshape(nb, seq, d)
```

```python
import math

import numpy as np

import jax
import jax.numpy as jnp
from jax import lax
from jax.experimental import pallas as pl
from jax.experimental.pallas import tpu as pltpu

D_MODEL = 1024
LRU_WIDTH = 512
LRU_BLOCKS = 8
LRU_BLOCK = LRU_WIDTH // LRU_BLOCKS
CONV_WIDTH = 4
LRU_C = 8.0
SB_WIDTH = 512
SB_HEADS = 8
SB_HEAD_DIM = SB_WIDTH // SB_HEADS
D_FF = 4 * D_MODEL
EPS = 1e-6

F32 = jnp.float32
BF16 = jnp.bfloat16

LANES = 128
SUBLANES = 8
VMEM_LIMIT_BYTES = 56 * 1024 * 1024

TT = 32
TQ = 512
TK = 128
HEADS_PER_STEP = LANES // SB_HEAD_DIM
KEY_TILES_PER_ITER = 2
LOG2E = math.log2(math.e)
MASKED_NZ = 1e30
F32_EXP_UNDERFLOW = -105.0
TM = 512


def _rms(x, g):
    ms = jnp.mean(x * x, axis=-1, keepdims=True)
    return x * lax.rsqrt(ms + EPS) * g


def _sigmoid(x):
    return 1.0 / (1.0 + jnp.exp(-x))


def _gelu_tanh(x):
    c = math.sqrt(2.0 / math.pi)
    return 0.5 * x * (1.0 + jnp.tanh(c * (x + 0.044715 * (x * x * x))))


def _mix_in_kernel(x_ref, perm_ref, permt_ref, g1_ref, win_ref, cw_ref, cb_ref, wa_ref, ba_ref, wx_ref, bx_ref,
                   lam_ref, gout_ref,
                   ylru_ref, q_ref, k_ref, v_ref,
                   xl_ext, h_state):
    nb = x_ref.shape[0]
    rows = nb * TT
    grows = perm_ref.shape[0]
    gsz = grows // TT
    n_groups = nb // gsz
    halo = (CONV_WIDTH - 1) * gsz

    @pl.when(pl.program_id(0) == 0)
    def _():
        xl_ext[:, 0:halo, :] = jnp.zeros((n_groups, halo, LRU_WIDTH), F32)
        h_state[...] = jnp.zeros_like(h_state)

    x = x_ref[...].reshape(rows, D_MODEL)
    xn = _rms(x, g1_ref[...]).astype(BF16)

    def proj(lhs, lo, width):
        return jnp.dot(lhs, win_ref[:, lo:lo + width], preferred_element_type=F32)

    def group(m, g):
        return m[g * grows:(g + 1) * grows, :]

    base = 2 * LRU_WIDTH

    xt = jnp.concatenate(
        [jnp.dot(perm_ref[...], group(xn, g), preferred_element_type=F32) for g in range(n_groups)],
        axis=0).astype(BF16)
    xl = proj(xt, 0, LRU_WIDTH)
    gate = proj(xt, LRU_WIDTH, LRU_WIDTH)

    q_ref[...] = (proj(xn, base, SB_WIDTH) * (-1.0 / math.sqrt(SB_HEAD_DIM))).astype(BF16).reshape(nb, TT, SB_WIDTH)

    cs = []
    for g in range(n_groups):
        xl_ext[g, halo:halo + grows, :] = group(xl, g)
        cg = cb_ref[...] + cw_ref[0:1, :] * xl_ext[g, 0:grows, :]
        for kk in range(1, CONV_WIDTH):
            cg = cg + cw_ref[kk:kk + 1, :] * xl_ext[g, kk * gsz:kk * gsz + grows, :]
        xl_ext[g, 0:halo, :] = xl_ext[g, grows:grows + halo, :]
        cs.append(cg)
    c = jnp.concatenate(cs, axis=0)

    cb16 = c.astype(BF16)
    r = _sigmoid(jnp.dot(cb16, wa_ref[...], preferred_element_type=F32) + ba_ref[...])
    i = _sigmoid(jnp.dot(cb16, wx_ref[...], preferred_element_type=F32) + bx_ref[...])

    k_ref[...] = proj(xn, base + SB_WIDTH, SB_WIDTH).astype(BF16).reshape(nb, TT, SB_WIDTH)
    v_ref[...] = proj(xn, base + 2 * SB_WIDTH, SB_WIDTH).astype(BF16).reshape(nb, TT, SB_WIDTH)
    lam = lam_ref[...]
    softplus_neg_lam = jnp.maximum(-lam, 0.0) + jnp.log(1.0 + jnp.exp(-jnp.abs(lam)))
    log_a = (-LRU_C * softplus_neg_lam) * r
    a = jnp.exp(log_a)
    bterm = jnp.sqrt(-jnp.tanh(log_a) * (a * a + 1.0)) * (i * c)

    h = [h_state[g] for g in range(n_groups)]
    hs = [[] for _ in range(n_groups)]
    for t in range(TT):
        for g in range(n_groups):
            r0 = g * grows + t * gsz
            h[g] = a[r0:r0 + gsz, :] * h[g] + bterm[r0:r0 + gsz, :]
            hs[g].append(h[g])
    for g in range(n_groups):
        h_state[g] = h[g]

    y = jnp.concatenate([hg for g in range(n_groups) for hg in hs[g]], axis=0) * _gelu_tanh(gate)
    yn = _rms(y, gout_ref[...]).astype(BF16)
    yb = jnp.concatenate(
        [jnp.dot(permt_ref[...], group(yn, g), preferred_element_type=F32) for g in range(n_groups)], axis=0)
    ylru_ref[...] = yb.astype(BF16).reshape(nb, TT, LRU_WIDTH)


def _time_major_perm(gsz):
    p = np.zeros((gsz * TT, gsz * TT), np.float32)
    for b in range(gsz):
        for t in range(TT):
            p[t * gsz + b, b * TT + t] = 1.0
    return p


def _mix_in(x, g1, win, cw, cb, wa, ba, wx, bx, lam, gout):
    nb, seq, _ = x.shape
    gsz = SUBLANES if nb % SUBLANES == 0 else nb
    grows = gsz * TT
    perm = _time_major_perm(gsz)
    const2 = lambda j: (0, 0)
    out_sds = jax.ShapeDtypeStruct((nb, seq, LRU_WIDTH), BF16)
    out_spec = pl.BlockSpec((nb, TT, LRU_WIDTH), lambda j: (0, j, 0))
    return pl.pallas_call(
        _mix_in_kernel,
        grid=(seq // TT,),
        in_specs=[
            pl.BlockSpec((nb, TT, D_MODEL), lambda j: (0, j, 0)),
            pl.BlockSpec((grows, grows), const2),
            pl.BlockSpec((grows, grows), const2),
            pl.BlockSpec((1, D_MODEL), const2),
            pl.BlockSpec(win.shape, const2),
            pl.BlockSpec(cw.shape, const2),
            pl.BlockSpec((1, LRU_WIDTH), const2),
            pl.BlockSpec(wa.shape, const2),
            pl.BlockSpec((1, LRU_WIDTH), const2),
            pl.BlockSpec(wx.shape, const2),
            pl.BlockSpec((1, LRU_WIDTH), const2),
            pl.BlockSpec((1, LRU_WIDTH), const2),
            pl.BlockSpec((1, LRU_WIDTH), const2),
        ],
        out_specs=[out_spec, out_spec, out_spec, out_spec],
        out_shape=[out_sds, out_sds, out_sds, out_sds],
        scratch_shapes=[
            pltpu.VMEM((nb // gsz, (CONV_WIDTH - 1) * gsz + grows, LRU_WIDTH), F32),
            pltpu.VMEM((nb // gsz, gsz, LRU_WIDTH), F32),
        ],
        compiler_params=pltpu.CompilerParams(
            dimension_semantics=("arbitrary",), vmem_limit_bytes=VMEM_LIMIT_BYTES),
        name="mix_in",
    )(x, jnp.asarray(perm, BF16), jnp.asarray(perm.T, BF16), g1, win, cw, cb, wa, ba, wx, bx, lam, gout)


def _attn_kernel(q_ref, k_ref, v_ref, o_ref, vt_s):
    qb = pl.program_id(2)
    n_kblocks = vt_s.shape[0]

    @pl.when(qb == 0)
    def _():
        for cblk in range(n_kblocks):
            blk = v_ref[0, cblk * TK:(cblk + 1) * TK, :].astype(F32)
            vt_s[cblk] = blk.T.astype(BF16)

    q = q_ref[0]
    lane = lax.broadcasted_iota(jnp.int32, (TQ, LANES), 1)
    qhs = [jnp.where((lane >= h * SB_HEAD_DIM) & (lane < (h + 1) * SB_HEAD_DIM), q, jnp.zeros_like(q))
           for h in range(HEADS_PER_STEP)]
    row_i = lax.broadcasted_iota(jnp.int32, (TK, TK), 0)
    col_i = lax.broadcasted_iota(jnp.int32, (TK, TK), 1)
    key_before_query = row_i < col_i
    upper = jnp.where(col_i >= row_i, 1.0, 0.0).astype(BF16)

    def run(tiles, state):
        work = [(n, h) for n in range(len(tiles)) for h in range(HEADS_PER_STEP)]
        nz = {}
        for (n, h) in work:
            kb, col0, diagonal, _ = tiles[n]
            kblk = k_ref[0, pl.ds(pl.multiple_of(kb * TK, TK), TK), :]
            v = lax.dot_general(kblk, qhs[h][col0:, :], (((1,), (1,)), ((), ())),
                                preferred_element_type=F32)
            if diagonal:
                first = jnp.where(key_before_query, v[:, :TK], MASKED_NZ)
                v = first if v.shape[1] == TK else jnp.concatenate([first, v[:, TK:]], axis=1)
            nz[n, h] = v
        surv = {}
        for (n, h) in work:
            e = jnp.exp2(jnp.abs(nz[n, h]) * (-LOG2E))
            log_1m_beta = jnp.minimum(nz[n, h], 0.0) - jnp.log(1.0 + e)
            surv[n, h] = jnp.dot(upper, log_1m_beta.astype(BF16), preferred_element_type=F32)
        out = []
        for h in range(HEADS_PER_STEP):
            carry, acc = state[2 * h], state[2 * h + 1]
            atts, vts = [], []
            for n, (kb, col0, _, bias) in enumerate(tiles):
                seen = carry[:, col0:]
                seen_b = seen if bias is None else seen + bias
                att = jnp.exp((surv[n, h] - nz[n, h]) + seen_b).astype(BF16)
                vt = vt_s[kb, h * SB_HEAD_DIM:(h + 1) * SB_HEAD_DIM, :]
                if col0 == 0:
                    atts.append(att)
                    vts.append(vt)
                    carry = carry + surv[n, h][0:1, :]
                else:
                    part = acc[:, col0:] + jnp.dot(vt, att, preferred_element_type=F32)
                    acc = jnp.concatenate([acc[:, :col0], part], axis=1)
                    carry = jnp.concatenate([carry[:, :col0], seen + surv[n, h][0:1, :]], axis=1)
            acc = acc + jnp.dot(jnp.concatenate(vts, axis=1), jnp.concatenate(atts, axis=0),
                                preferred_element_type=F32)
            out += [carry, acc]
        return tuple(out)

    def some_query_alive(state):
        top = state[0]
        for head in range(1, HEADS_PER_STEP):
            top = jnp.maximum(top, state[2 * head])
        return jnp.max(top) >= F32_EXP_UNDERFLOW

    n_diag = TQ // TK
    state = []
    for head in range(HEADS_PER_STEP):
        state += [jnp.zeros((1, TQ), F32), jnp.zeros((SB_HEAD_DIM, TQ), F32)]
    no_earlier = jnp.where(qb > 0, 0.0, -MASKED_NZ).astype(F32)
    tiles = [(qb * n_diag + d, d * TK, True, None) for d in reversed(range(n_diag))]
    tiles += [(jnp.maximum(qb * n_diag - 1 - u, 0), 0, False, no_earlier) for u in range(KEY_TILES_PER_ITER)]
    state = run(tiles, tuple(state))

    n_steps = qb * (n_diag // KEY_TILES_PER_ITER)

    def body(loop_state):
        i, state = loop_state[0], loop_state[2:]
        kb = qb * n_diag - 1 - KEY_TILES_PER_ITER * i
        state = run([(kb - u, 0, False, None) for u in range(KEY_TILES_PER_ITER)], state)
        return (i + 1, jnp.logical_and(i + 1 < n_steps, some_query_alive(state))) + state

    go = jnp.logical_and(1 < n_steps, some_query_alive(state))
    state = lax.while_loop(lambda loop_state: loop_state[1], body, (jnp.int32(1), go) + state)[2:]
    accs = [state[2 * head + 1] for head in range(HEADS_PER_STEP)]
    o_ref[0] = jnp.concatenate(accs, axis=0).T


def _attn(q, k, v):
    nb, seq, _ = q.shape
    n_slabs = SB_WIDTH // LANES
    return pl.pallas_call(
        _attn_kernel,
        grid=(nb, n_slabs, seq // TQ),
        in_specs=[
            pl.BlockSpec((1, TQ, LANES), lambda b, s, i: (b, i, s)),
            pl.BlockSpec((1, seq, LANES), lambda b, s, i: (b, 0, s)),
            pl.BlockSpec((1, seq, LANES), lambda b, s, i: (b, 0, s)),
        ],
        out_specs=pl.BlockSpec((1, TQ, LANES), lambda b, s, i: (b, i, s)),
        out_shape=jax.ShapeDtypeStruct((nb, seq, SB_WIDTH), F32),
        scratch_shapes=[pltpu.VMEM((seq // TK, LANES, TK), BF16)],
        compiler_params=pltpu.CompilerParams(
            dimension_semantics=("arbitrary", "arbitrary", "arbitrary"),
            vmem_limit_bytes=VMEM_LIMIT_BYTES),
        name="sb_attn",
    )(q, k, v)


def _mix_out_kernel(x_ref, ylru_ref, ysb_ref, gsb_ref, wout_ref, g2_ref, wup_ref, wdn_ref, gf_ref, o_ref):
    ysb = _rms(ysb_ref[...], gsb_ref[...]).astype(BF16)
    h = x_ref[...]
    h = h + jnp.dot(ylru_ref[...], wout_ref[0:LRU_WIDTH, :], preferred_element_type=F32)
    h = h + jnp.dot(ysb, wout_ref[LRU_WIDTH:LRU_WIDTH + SB_WIDTH, :], preferred_element_type=F32)
    hn = _rms(h, g2_ref[...]).astype(BF16)
    up = jnp.maximum(jnp.dot(hn, wup_ref[...], preferred_element_type=F32), 0.0)
    h = h + jnp.dot((up * up).astype(BF16), wdn_ref[...], preferred_element_type=F32)
    o_ref[...] = _rms(h, gf_ref[...])


def _mix_out(x2, ylru2, ysb2, gsb, wout, g2, wup, wdn, gf):
    m = x2.shape[0]
    const2 = lambda i: (0, 0)
    row = lambda i: (i, 0)
    resident = dict(pipeline_mode=pl.Buffered(1))
    return pl.pallas_call(
        _mix_out_kernel,
        grid=(m // TM,),
        in_specs=[
            pl.BlockSpec((TM, D_MODEL), row),
            pl.BlockSpec((TM, LRU_WIDTH), row),
            pl.BlockSpec((TM, SB_WIDTH), row),
            pl.BlockSpec((1, SB_WIDTH), const2),
            pl.BlockSpec(wout.shape, const2, **resident),
            pl.BlockSpec((1, D_MODEL), const2),
            pl.BlockSpec(wup.shape, const2, **resident),
            pl.BlockSpec(wdn.shape, const2, **resident),
            pl.BlockSpec((1, D_MODEL), const2),
        ],
        out_specs=pl.BlockSpec((TM, D_MODEL), row),
        out_shape=jax.ShapeDtypeStruct((m, D_MODEL), F32),
        compiler_params=pltpu.CompilerParams(
            dimension_semantics=("arbitrary",), vmem_limit_bytes=VMEM_LIMIT_BYTES),
        name="mix_out",
    )(x2, ylru2, ysb2, gsb, wout, g2, wup, wdn, gf)


def _block_diag(w):
    nblk, n, _ = w.shape
    eye = jnp.eye(nblk, dtype=w.dtype)
    return (eye[:, None, :, None] * w[:, :, None, :]).reshape(nblk * n, nblk * n)


def kernel(x, norm1_g, w_in, conv_w, conv_b, lru_w_a, lru_b_a, lru_w_x, lru_b_x, lru_lambda,
           lru_out_g, sb_out_g, w_out, norm2_g, w_up, w_down, final_g):
    nb, seq, d = x.shape
    assert w_in.shape[0] == 1, "kernel supports depth 1 only"
    layer = 0
    ylru, q, k, v = _mix_in(
        x, norm1_g[layer].reshape(1, d), w_in[layer].astype(BF16),
        conv_w[layer], conv_b[layer].reshape(1, LRU_WIDTH),
        _block_diag(lru_w_a[layer]).astype(BF16), lru_b_a[layer].reshape(1, LRU_WIDTH),
        _block_diag(lru_w_x[layer]).astype(BF16), lru_b_x[layer].reshape(1, LRU_WIDTH),
        lru_lambda[layer].reshape(1, LRU_WIDTH), lru_out_g[layer].reshape(1, LRU_WIDTH))
    ysb = _attn(q, k, v)
    out = _mix_out(
        x.reshape(nb * seq, d), ylru.reshape(nb * seq, LRU_WIDTH), ysb.reshape(nb * seq, SB_WIDTH),
        sb_out_g[layer].reshape(1, SB_WIDTH), w_out[layer].astype(BF16),
        norm2_g[layer].reshape(1, d),
        w_up[layer].astype(BF16), w_down[layer].astype(BF16),
        final_g.reshape(1, d))
    return out.reshape(nb, seq, d)
```

```python
import math

import numpy as np

import jax
import jax.numpy as jnp
from jax import lax
from jax.experimental import pallas as pl
from jax.experimental.pallas import tpu as pltpu

D_MODEL = 1024
LRU_WIDTH = 512
LRU_BLOCKS = 8
LRU_BLOCK = LRU_WIDTH // LRU_BLOCKS
CONV_WIDTH = 4
LRU_C = 8.0
SB_WIDTH = 512
SB_HEADS = 8
SB_HEAD_DIM = SB_WIDTH // SB_HEADS
D_FF = 4 * D_MODEL
EPS = 1e-6

F32 = jnp.float32
BF16 = jnp.bfloat16

LANES = 128
SUBLANES = 8
VMEM_LIMIT_BYTES = 56 * 1024 * 1024

TT = 32
TQ = 512
TK = 128
SLAB = 256
HEADS_PER_STEP = SLAB // SB_HEAD_DIM
HEADS_PER_GROUP = 2
KEY_TILES_PER_ITER = 2
LOG2E = math.log2(math.e)
MASKED_NZ = 1e30
F32_EXP_UNDERFLOW = -105.0
TM = 512


def _rms(x, g):
    ms = jnp.mean(x * x, axis=-1, keepdims=True)
    return x * lax.rsqrt(ms + EPS) * g


def _sigmoid(x):
    return 1.0 / (1.0 + jnp.exp(-x))


def _gelu_tanh(x):
    c = math.sqrt(2.0 / math.pi)
    return 0.5 * x * (1.0 + jnp.tanh(c * (x + 0.044715 * (x * x * x))))


def _mix_in_kernel(x_ref, perm_ref, permt_ref, g1_ref, win_ref, cw_ref, cb_ref, wa_ref, ba_ref, wx_ref, bx_ref,
                   lam_ref, gout_ref,
                   ylru_ref, q_ref, k_ref, v_ref,
                   xl_ext, h_state):
    nb = x_ref.shape[0]
    rows = nb * TT
    grows = perm_ref.shape[0]
    gsz = grows // TT
    n_groups = nb // gsz
    halo = (CONV_WIDTH - 1) * gsz

    @pl.when(pl.program_id(0) == 0)
    def _():
        xl_ext[:, 0:halo, :] = jnp.zeros((n_groups, halo, LRU_WIDTH), F32)
        h_state[...] = jnp.zeros_like(h_state)

    x = x_ref[...].reshape(rows, D_MODEL)
    xn = _rms(x, g1_ref[...]).astype(BF16)

    def proj(lhs, lo, width):
        return jnp.dot(lhs, win_ref[:, lo:lo + width], preferred_element_type=F32)

    def group(m, g):
        return m[g * grows:(g + 1) * grows, :]

    base = 2 * LRU_WIDTH
    groups = range(n_groups)

    xt = [None] * n_groups
    xl = [None] * n_groups
    for g in groups:
        xt[g] = jnp.dot(perm_ref[...], group(xn, g), preferred_element_type=F32).astype(BF16)
        xl[g] = proj(xt[g], 0, LRU_WIDTH)

    c = [None] * n_groups
    r = [None] * n_groups
    i = [None] * n_groups
    gate = [None] * n_groups
    for g in groups:
        xl_ext[g, halo:halo + grows, :] = xl[g]
        cg = cb_ref[...] + cw_ref[0:1, :] * xl_ext[g, 0:grows, :]
        for kk in range(1, CONV_WIDTH):
            cg = cg + cw_ref[kk:kk + 1, :] * xl_ext[g, kk * gsz:kk * gsz + grows, :]
        xl_ext[g, 0:halo, :] = xl_ext[g, grows:grows + halo, :]
        c[g] = cg
        cb16 = cg.astype(BF16)
        r[g] = _sigmoid(jnp.dot(cb16, wa_ref[...], preferred_element_type=F32) + ba_ref[...])
        i[g] = _sigmoid(jnp.dot(cb16, wx_ref[...], preferred_element_type=F32) + bx_ref[...])
        gate[g] = proj(xt[g], LRU_WIDTH, LRU_WIDTH)
        if g == 0:
            q_ref[...] = (proj(xn, base, SB_WIDTH) * (-1.0 / math.sqrt(SB_HEAD_DIM))).astype(BF16).reshape(
                nb, TT, SB_WIDTH)
    k_ref[...] = proj(xn, base + SB_WIDTH, SB_WIDTH).astype(BF16).reshape(nb, TT, SB_WIDTH)
    v_ref[...] = proj(xn, base + 2 * SB_WIDTH, SB_WIDTH).astype(BF16).reshape(nb, TT, SB_WIDTH)

    lam = lam_ref[...]
    softplus_neg_lam = jnp.maximum(-lam, 0.0) + jnp.log(1.0 + jnp.exp(-jnp.abs(lam)))
    yb = [None] * n_groups
    for g in groups:
        log_a = (-LRU_C * softplus_neg_lam) * r[g]
        a = jnp.exp(log_a)
        bterm = jnp.sqrt(-jnp.tanh(log_a) * (a * a + 1.0)) * (i[g] * c[g])
        h = h_state[g]
        hs = []
        for t in range(TT):
            h = a[t * gsz:(t + 1) * gsz, :] * h + bterm[t * gsz:(t + 1) * gsz, :]
            hs.append(h)
        h_state[g] = h
        y = jnp.concatenate(hs, axis=0) * _gelu_tanh(gate[g])
        yn = _rms(y, gout_ref[...]).astype(BF16)
        yb[g] = jnp.dot(permt_ref[...], yn, preferred_element_type=F32).astype(BF16)
    ylru_ref[...] = jnp.concatenate(yb, axis=0).reshape(nb, TT, LRU_WIDTH)


def _time_major_perm(gsz):
    p = np.zeros((gsz * TT, gsz * TT), np.float32)
    for b in range(gsz):
        for t in range(TT):
            p[t * gsz + b, b * TT + t] = 1.0
    return p


def _mix_in(x, g1, win, cw, cb, wa, ba, wx, bx, lam, gout):
    nb, seq, _ = x.shape
    gsz = SUBLANES if nb % SUBLANES == 0 else nb
    grows = gsz * TT
    perm = _time_major_perm(gsz)
    const2 = lambda j: (0, 0)
    out_sds = jax.ShapeDtypeStruct((nb, seq, LRU_WIDTH), BF16)
    out_spec = pl.BlockSpec((nb, TT, LRU_WIDTH), lambda j: (0, j, 0))
    return pl.pallas_call(
        _mix_in_kernel,
        grid=(seq // TT,),
        in_specs=[
            pl.BlockSpec((nb, TT, D_MODEL), lambda j: (0, j, 0)),
            pl.BlockSpec((grows, grows), const2),
            pl.BlockSpec((grows, grows), const2),
            pl.BlockSpec((1, D_MODEL), const2),
            pl.BlockSpec(win.shape, const2),
            pl.BlockSpec(cw.shape, const2),
            pl.BlockSpec((1, LRU_WIDTH), const2),
            pl.BlockSpec(wa.shape, const2),
            pl.BlockSpec((1, LRU_WIDTH), const2),
            pl.BlockSpec(wx.shape, const2),
            pl.BlockSpec((1, LRU_WIDTH), const2),
            pl.BlockSpec((1, LRU_WIDTH), const2),
            pl.BlockSpec((1, LRU_WIDTH), const2),
        ],
        out_specs=[out_spec, out_spec, out_spec, out_spec],
        out_shape=[out_sds, out_sds, out_sds, out_sds],
        scratch_shapes=[
            pltpu.VMEM((nb // gsz, (CONV_WIDTH - 1) * gsz + grows, LRU_WIDTH), F32),
            pltpu.VMEM((nb // gsz, gsz, LRU_WIDTH), F32),
        ],
        compiler_params=pltpu.CompilerParams(
            dimension_semantics=("arbitrary",), vmem_limit_bytes=VMEM_LIMIT_BYTES),
        name="mix_in",
    )(x, jnp.asarray(perm, BF16), jnp.asarray(perm.T, BF16), g1, win, cw, cb, wa, ba, wx, bx, lam, gout)


def _attn_kernel(q_ref, k_ref, v_ref, o_ref, vt_s):
    qb = pl.program_id(2)
    n_kblocks = vt_s.shape[0]

    @pl.when(qb == 0)
    def _():
        for cblk in range(n_kblocks):
            blk = v_ref[0, cblk * TK:(cblk + 1) * TK, :].astype(F32)
            vt_s[cblk] = blk.T.astype(BF16)

    q = q_ref[0]
    lane = lax.broadcasted_iota(jnp.int32, (TQ, SLAB), 1)
    qhs = [jnp.where((lane >= h * SB_HEAD_DIM) & (lane < (h + 1) * SB_HEAD_DIM), q, jnp.zeros_like(q))
           for h in range(HEADS_PER_STEP)]
    row_i = lax.broadcasted_iota(jnp.int32, (TK, TK), 0)
    col_i = lax.broadcasted_iota(jnp.int32, (TK, TK), 1)
    key_before_query = row_i < col_i
    upper = jnp.where(col_i >= row_i, 1.0, 0.0).astype(BF16)

    def run(tiles, state):
        nz, surv, out = {}, {}, {}

        def scores(heads):
            for n, (kb, col0, diagonal, _) in enumerate(tiles):
                for h in heads:
                    kblk = k_ref[0, pl.ds(pl.multiple_of(kb * TK, TK), TK), :]
                    v = lax.dot_general(kblk, qhs[h][col0:, :], (((1,), (1,)), ((), ())),
                                        preferred_element_type=F32)
                    if diagonal:
                        first = jnp.where(key_before_query, v[:, :TK], MASKED_NZ)
                        v = first if v.shape[1] == TK else jnp.concatenate([first, v[:, TK:]], axis=1)
                    nz[n, h] = v

        def survival(heads):
            for n in range(len(tiles)):
                for h in heads:
                    e = jnp.exp2(jnp.abs(nz[n, h]) * (-LOG2E))
                    log_1m_beta = jnp.minimum(nz[n, h], 0.0) - jnp.log(1.0 + e)
                    surv[n, h] = jnp.dot(upper, log_1m_beta.astype(BF16), preferred_element_type=F32)

        def weigh(heads):
            for h in heads:
                carry, acc = state[2 * h], state[2 * h + 1]
                atts, vts = [], []
                for n, (kb, col0, _, bias) in enumerate(tiles):
                    seen = carry[:, col0:]
                    seen_b = seen if bias is None else seen + bias
                    att = jnp.exp((surv[n, h] - nz[n, h]) + seen_b).astype(BF16)
                    vt = vt_s[kb, h * SB_HEAD_DIM:(h + 1) * SB_HEAD_DIM, :]
                    if col0 == 0:
                        atts.append(att)
                        vts.append(vt)
                        carry = carry + surv[n, h][0:1, :]
                    else:
                        part = acc[:, col0:] + jnp.dot(vt, att, preferred_element_type=F32)
                        acc = jnp.concatenate([acc[:, :col0], part], axis=1)
                        carry = jnp.concatenate([carry[:, :col0], seen + surv[n, h][0:1, :]], axis=1)
                acc = acc + jnp.dot(jnp.concatenate(vts, axis=1), jnp.concatenate(atts, axis=0),
                                    preferred_element_type=F32)
                out[h] = (carry, acc)

        groups = [list(range(g, g + HEADS_PER_GROUP)) for g in range(0, HEADS_PER_STEP, HEADS_PER_GROUP)]
        scores(groups[0])
        survival(groups[0])
        for prev, cur in zip(groups[:-1], groups[1:]):
            scores(cur)
            weigh(prev)
            survival(cur)
        weigh(groups[-1])
        return tuple(x for h in range(HEADS_PER_STEP) for x in out[h])

    def some_query_alive(state):
        top = state[0]
        for head in range(1, HEADS_PER_STEP):
            top = jnp.maximum(top, state[2 * head])
        return jnp.max(top) >= F32_EXP_UNDERFLOW

    n_diag = TQ // TK
    state = []
    for head in range(HEADS_PER_STEP):
        state += [jnp.zeros((1, TQ), F32), jnp.zeros((SB_HEAD_DIM, TQ), F32)]
    no_earlier = jnp.where(qb > 0, 0.0, -MASKED_NZ).astype(F32)
    tiles = [(qb * n_diag + d, d * TK, True, None) for d in reversed(range(n_diag))]
    tiles += [(jnp.maximum(qb * n_diag - 1 - u, 0), 0, False, no_earlier) for u in range(KEY_TILES_PER_ITER)]
    state = run(tiles, tuple(state))

    n_steps = qb * (n_diag // KEY_TILES_PER_ITER)

    def body(loop_state):
        i, state = loop_state[0], loop_state[2:]
        kb = qb * n_diag - 1 - KEY_TILES_PER_ITER * i
        state = run([(kb - u, 0, False, None) for u in range(KEY_TILES_PER_ITER)], state)
        return (i + 1, jnp.logical_and(i + 1 < n_steps, some_query_alive(state))) + state

    go = jnp.logical_and(1 < n_steps, some_query_alive(state))
    state = lax.while_loop(lambda loop_state: loop_state[1], body, (jnp.int32(1), go) + state)[2:]
    accs = [state[2 * head + 1] for head in range(HEADS_PER_STEP)]
    o_ref[0] = jnp.concatenate(accs, axis=0).T


def _attn(q, k, v):
    nb, seq, _ = q.shape
    n_slabs = SB_WIDTH // SLAB
    return pl.pallas_call(
        _attn_kernel,
        grid=(nb, n_slabs, seq // TQ),
        in_specs=[
            pl.BlockSpec((1, TQ, SLAB), lambda b, s, i: (b, i, s)),
            pl.BlockSpec((1, seq, SLAB), lambda b, s, i: (b, 0, s)),
            pl.BlockSpec((1, seq, SLAB), lambda b, s, i: (b, 0, s)),
        ],
        out_specs=pl.BlockSpec((1, TQ, SLAB), lambda b, s, i: (b, i, s)),
        out_shape=jax.ShapeDtypeStruct((nb, seq, SB_WIDTH), F32),
        scratch_shapes=[pltpu.VMEM((seq // TK, SLAB, TK), BF16)],
        compiler_params=pltpu.CompilerParams(
            dimension_semantics=("arbitrary", "arbitrary", "arbitrary"),
            vmem_limit_bytes=VMEM_LIMIT_BYTES),
        name="sb_attn",
    )(q, k, v)


def _mix_out_kernel(x_ref, ylru_ref, ysb_ref, gsb_ref, wout_ref, g2_ref, wup_ref, wdn_ref, gf_ref, o_ref):
    ysb = _rms(ysb_ref[...], gsb_ref[...]).astype(BF16)
    h = x_ref[...]
    h = h + jnp.dot(ylru_ref[...], wout_ref[0:LRU_WIDTH, :], preferred_element_type=F32)
    h = h + jnp.dot(ysb, wout_ref[LRU_WIDTH:LRU_WIDTH + SB_WIDTH, :], preferred_element_type=F32)
    hn = _rms(h, g2_ref[...]).astype(BF16)
    up = jnp.maximum(jnp.dot(hn, wup_ref[...], preferred_element_type=F32), 0.0)
    h = h + jnp.dot((up * up).astype(BF16), wdn_ref[...], preferred_element_type=F32)
    o_ref[...] = _rms(h, gf_ref[...])


def _mix_out(x2, ylru2, ysb2, gsb, wout, g2, wup, wdn, gf):
    m = x2.shape[0]
    const2 = lambda i: (0, 0)
    row = lambda i: (i, 0)
    resident = dict(pipeline_mode=pl.Buffered(1))
    return pl.pallas_call(
        _mix_out_kernel,
        grid=(m // TM,),
        in_specs=[
            pl.BlockSpec((TM, D_MODEL), row),
            pl.BlockSpec((TM, LRU_WIDTH), row),
            pl.BlockSpec((TM, SB_WIDTH), row),
            pl.BlockSpec((1, SB_WIDTH), const2),
            pl.BlockSpec(wout.shape, const2, **resident),
            pl.BlockSpec((1, D_MODEL), const2),
            pl.BlockSpec(wup.shape, const2, **resident),
            pl.BlockSpec(wdn.shape, const2, **resident),
            pl.BlockSpec((1, D_MODEL), const2),
        ],
        out_specs=pl.BlockSpec((TM, D_MODEL), row),
        out_shape=jax.ShapeDtypeStruct((m, D_MODEL), F32),
        compiler_params=pltpu.CompilerParams(
            dimension_semantics=("arbitrary",), vmem_limit_bytes=VMEM_LIMIT_BYTES),
        name="mix_out",
    )(x2, ylru2, ysb2, gsb, wout, g2, wup, wdn, gf)


def _block_diag(w):
    nblk, n, _ = w.shape
    eye = jnp.eye(nblk, dtype=w.dtype)
    return (eye[:, None, :, None] * w[:, :, None, :]).reshape(nblk * n, nblk * n)


def kernel(x, norm1_g, w_in, conv_w, conv_b, lru_w_a, lru_b_a, lru_w_x, lru_b_x, lru_lambda,
           lru_out_g, sb_out_g, w_out, norm2_g, w_up, w_down, final_g):
    nb, seq, d = x.shape
    assert w_in.shape[0] == 1, "kernel supports depth 1 only"
    layer = 0
    ylru, q, k, v = _mix_in(
        x, norm1_g[layer].reshape(1, d), w_in[layer].astype(BF16),
        conv_w[layer], conv_b[layer].reshape(1, LRU_WIDTH),
        _block_diag(lru_w_a[layer]).astype(BF16), lru_b_a[layer].reshape(1, LRU_WIDTH),
        _block_diag(lru_w_x[layer]).astype(BF16), lru_b_x[layer].reshape(1, LRU_WIDTH),
        lru_lambda[layer].reshape(1, LRU_WIDTH), lru_out_g[layer].reshape(1, LRU_WIDTH))
    ysb = _attn(q, k, v)
    out = _mix_out(
        x.reshape(nb * seq, d), ylru.reshape(nb * seq, LRU_WIDTH), ysb.reshape(nb * seq, SB_WIDTH),
        sb_out_g[layer].reshape(1, SB_WIDTH), w_out[layer].astype(BF16),
        norm2_g[layer].reshape(1, d),
        w_up[layer].astype(BF16), w_down[layer].astype(BF16),
        final_g.reshape(1, d))
    return out.reshape(nb, seq, d)
```

```python
import math

import numpy as np

import jax
import jax.numpy as jnp
from jax import lax
from jax.experimental import pallas as pl
from jax.experimental.pallas import tpu as pltpu

D_MODEL = 1024
LRU_WIDTH = 512
LRU_BLOCKS = 8
LRU_BLOCK = LRU_WIDTH // LRU_BLOCKS
CONV_WIDTH = 4
LRU_C = 8.0
SB_WIDTH = 512
SB_HEADS = 8
SB_HEAD_DIM = SB_WIDTH // SB_HEADS
D_FF = 4 * D_MODEL
EPS = 1e-6

F32 = jnp.float32
BF16 = jnp.bfloat16

LANES = 128
SUBLANES = 8
VMEM_LIMIT_BYTES = 56 * 1024 * 1024

TT = 32
TQ = 256
TK = 128
SLAB = 256
HEADS_PER_STEP = SLAB // SB_HEAD_DIM
HEADS_PER_GROUP = 4
KEY_TILES_PER_ITER = 2
LOG2E = math.log2(math.e)
MASKED_NZ = 1e30
F32_EXP_UNDERFLOW = -105.0
TM = 512


def _rms(x, g):
    ms = jnp.mean(x * x, axis=-1, keepdims=True)
    return x * lax.rsqrt(ms + EPS) * g


def _sigmoid(x):
    return 1.0 / (1.0 + jnp.exp(-x))


def _gelu_tanh(x):
    c = math.sqrt(2.0 / math.pi)
    return 0.5 * x * (1.0 + jnp.tanh(c * (x + 0.044715 * (x * x * x))))


def _mix_in_kernel(x_ref, perm_ref, permt_ref, g1_ref, win_ref, cw_ref, cb_ref, wa_ref, ba_ref, wx_ref, bx_ref,
                   lam_ref, gout_ref,
                   ylru_ref, q_ref, k_ref, v_ref,
                   xl_ext, h_state):
    nb = x_ref.shape[0]
    rows = nb * TT
    grows = perm_ref.shape[0]
    gsz = grows // TT
    n_groups = nb // gsz
    halo = (CONV_WIDTH - 1) * gsz

    @pl.when(pl.program_id(0) == 0)
    def _():
        xl_ext[:, 0:halo, :] = jnp.zeros((n_groups, halo, LRU_WIDTH), F32)
        h_state[...] = jnp.zeros_like(h_state)

    x = x_ref[...].reshape(rows, D_MODEL)
    xn = _rms(x, g1_ref[...]).astype(BF16)

    def proj(lhs, lo, width):
        return jnp.dot(lhs, win_ref[:, lo:lo + width], preferred_element_type=F32)

    def group(m, g):
        return m[g * grows:(g + 1) * grows, :]

    base = 2 * LRU_WIDTH

    xt = jnp.concatenate(
        [jnp.dot(perm_ref[...], group(xn, g), preferred_element_type=F32) for g in range(n_groups)],
        axis=0).astype(BF16)
    xl = proj(xt, 0, LRU_WIDTH)
    gate = proj(xt, LRU_WIDTH, LRU_WIDTH)

    q_ref[...] = (proj(xn, base, SB_WIDTH) * (-1.0 / math.sqrt(SB_HEAD_DIM))).astype(BF16).reshape(nb, TT, SB_WIDTH)

    cs = []
    for g in range(n_groups):
        xl_ext[g, halo:halo + grows, :] = group(xl, g)
        cg = cb_ref[...] + cw_ref[0:1, :] * xl_ext[g, 0:grows, :]
        for kk in range(1, CONV_WIDTH):
            cg = cg + cw_ref[kk:kk + 1, :] * xl_ext[g, kk * gsz:kk * gsz + grows, :]
        xl_ext[g, 0:halo, :] = xl_ext[g, grows:grows + halo, :]
        cs.append(cg)
    c = jnp.concatenate(cs, axis=0)

    cb16 = c.astype(BF16)
    r = _sigmoid(jnp.dot(cb16, wa_ref[...], preferred_element_type=F32) + ba_ref[...])
    i = _sigmoid(jnp.dot(cb16, wx_ref[...], preferred_element_type=F32) + bx_ref[...])

    k_ref[...] = proj(xn, base + SB_WIDTH, SB_WIDTH).astype(BF16).reshape(nb, TT, SB_WIDTH)
    v_ref[...] = proj(xn, base + 2 * SB_WIDTH, SB_WIDTH).astype(BF16).reshape(nb, TT, SB_WIDTH)
    lam = lam_ref[...]
    softplus_neg_lam = jnp.maximum(-lam, 0.0) + jnp.log(1.0 + jnp.exp(-jnp.abs(lam)))
    log_a = (-LRU_C * softplus_neg_lam) * r
    a = jnp.exp(log_a)
    bterm = jnp.sqrt(-jnp.tanh(log_a) * (a * a + 1.0)) * (i * c)

    h = [h_state[g] for g in range(n_groups)]
    hs = [[] for _ in range(n_groups)]
    for t in range(TT):
        for g in range(n_groups):
            r0 = g * grows + t * gsz
            h[g] = a[r0:r0 + gsz, :] * h[g] + bterm[r0:r0 + gsz, :]
            hs[g].append(h[g])
    for g in range(n_groups):
        h_state[g] = h[g]

    y = jnp.concatenate([hg for g in range(n_groups) for hg in hs[g]], axis=0) * _gelu_tanh(gate)
    yn = _rms(y, gout_ref[...]).astype(BF16)
    yb = jnp.concatenate(
        [jnp.dot(permt_ref[...], group(yn, g), preferred_element_type=F32) for g in range(n_groups)], axis=0)
    ylru_ref[...] = yb.astype(BF16).reshape(nb, TT, LRU_WIDTH)


def _time_major_perm(gsz):
    p = np.zeros((gsz * TT, gsz * TT), np.float32)
    for b in range(gsz):
        for t in range(TT):
            p[t * gsz + b, b * TT + t] = 1.0
    return p


def _mix_in(x, g1, win, cw, cb, wa, ba, wx, bx, lam, gout):
    nb, seq, _ = x.shape
    gsz = SUBLANES if nb % SUBLANES == 0 else nb
    grows = gsz * TT
    perm = _time_major_perm(gsz)
    const2 = lambda j: (0, 0)
    out_sds = jax.ShapeDtypeStruct((nb, seq, LRU_WIDTH), BF16)
    out_spec = pl.BlockSpec((nb, TT, LRU_WIDTH), lambda j: (0, j, 0))
    return pl.pallas_call(
        _mix_in_kernel,
        grid=(seq // TT,),
        in_specs=[
            pl.BlockSpec((nb, TT, D_MODEL), lambda j: (0, j, 0)),
            pl.BlockSpec((grows, grows), const2),
            pl.BlockSpec((grows, grows), const2),
            pl.BlockSpec((1, D_MODEL), const2),
            pl.BlockSpec(win.shape, const2),
            pl.BlockSpec(cw.shape, const2),
            pl.BlockSpec((1, LRU_WIDTH), const2),
            pl.BlockSpec(wa.shape, const2),
            pl.BlockSpec((1, LRU_WIDTH), const2),
            pl.BlockSpec(wx.shape, const2),
            pl.BlockSpec((1, LRU_WIDTH), const2),
            pl.BlockSpec((1, LRU_WIDTH), const2),
            pl.BlockSpec((1, LRU_WIDTH), const2),
        ],
        out_specs=[out_spec, out_spec, out_spec, out_spec],
        out_shape=[out_sds, out_sds, out_sds, out_sds],
        scratch_shapes=[
            pltpu.VMEM((nb // gsz, (CONV_WIDTH - 1) * gsz + grows, LRU_WIDTH), F32),
            pltpu.VMEM((nb // gsz, gsz, LRU_WIDTH), F32),
        ],
        compiler_params=pltpu.CompilerParams(
            dimension_semantics=("arbitrary",), vmem_limit_bytes=VMEM_LIMIT_BYTES),
        name="mix_in",
    )(x, jnp.asarray(perm, BF16), jnp.asarray(perm.T, BF16), g1, win, cw, cb, wa, ba, wx, bx, lam, gout)


def _attn_kernel(q_ref, k_ref, v_ref, o_ref, vt_s):
    qb = pl.program_id(2)
    n_kblocks = vt_s.shape[0]

    @pl.when(qb == 0)
    def _():
        for cblk in range(n_kblocks):
            blk = v_ref[0, cblk * TK:(cblk + 1) * TK, :].astype(F32)
            vt_s[cblk] = blk.T.astype(BF16)

    q = q_ref[0]
    lane = lax.broadcasted_iota(jnp.int32, (TQ, SLAB), 1)
    qhs = [jnp.where((lane >= h * SB_HEAD_DIM) & (lane < (h + 1) * SB_HEAD_DIM), q, jnp.zeros_like(q))
           for h in range(HEADS_PER_STEP)]
    row_i = lax.broadcasted_iota(jnp.int32, (TK, TK), 0)
    col_i = lax.broadcasted_iota(jnp.int32, (TK, TK), 1)
    key_before_query = row_i < col_i
    upper = jnp.where(col_i >= row_i, 1.0, 0.0).astype(BF16)

    def run(tiles, state):
        nz, surv, out = {}, {}, {}

        def scores(heads):
            for n, (kb, col0, diagonal, _) in enumerate(tiles):
                for h in heads:
                    kblk = k_ref[0, pl.ds(pl.multiple_of(kb * TK, TK), TK), :]
                    v = lax.dot_general(kblk, qhs[h][col0:, :], (((1,), (1,)), ((), ())),
                                        preferred_element_type=F32)
                    if diagonal:
                        first = jnp.where(key_before_query, v[:, :TK], MASKED_NZ)
                        v = first if v.shape[1] == TK else jnp.concatenate([first, v[:, TK:]], axis=1)
                    nz[n, h] = v

        def survival(heads):
            for n in range(len(tiles)):
                for h in heads:
                    e = jnp.exp2(jnp.abs(nz[n, h]) * (-LOG2E))
                    log_1m_beta = jnp.minimum(nz[n, h], 0.0) - jnp.log(1.0 + e)
                    surv[n, h] = jnp.dot(upper, log_1m_beta.astype(BF16), preferred_element_type=F32)

        def weigh(heads):
            for h in heads:
                carry, acc = state[2 * h], state[2 * h + 1]
                atts, vts = [], []
                for n, (kb, col0, _, bias) in enumerate(tiles):
                    seen = carry[:, col0:]
                    seen_b = seen if bias is None else seen + bias
                    att = jnp.exp((surv[n, h] - nz[n, h]) + seen_b).astype(BF16)
                    vt = vt_s[kb, h * SB_HEAD_DIM:(h + 1) * SB_HEAD_DIM, :]
                    if col0 == 0:
                        atts.append(att)
                        vts.append(vt)
                        carry = carry + surv[n, h][0:1, :]
                    else:
                        part = acc[:, col0:] + jnp.dot(vt, att, preferred_element_type=F32)
                        acc = jnp.concatenate([acc[:, :col0], part], axis=1)
                        carry = jnp.concatenate([carry[:, :col0], seen + surv[n, h][0:1, :]], axis=1)
                acc = acc + jnp.dot(jnp.concatenate(vts, axis=1), jnp.concatenate(atts, axis=0),
                                    preferred_element_type=F32)
                out[h] = (carry, acc)

        groups = [list(range(g, g + HEADS_PER_GROUP)) for g in range(0, HEADS_PER_STEP, HEADS_PER_GROUP)]
        scores(groups[0])
        survival(groups[0])
        for prev, cur in zip(groups[:-1], groups[1:]):
            scores(cur)
            weigh(prev)
            survival(cur)
        weigh(groups[-1])
        return tuple(x for h in range(HEADS_PER_STEP) for x in out[h])

    def some_query_alive(state):
        top = state[0]
        for head in range(1, HEADS_PER_STEP):
            top = jnp.maximum(top, state[2 * head])
        return jnp.max(top) >= F32_EXP_UNDERFLOW

    n_diag = TQ // TK
    state = []
    for head in range(HEADS_PER_STEP):
        state += [jnp.zeros((1, TQ), F32), jnp.zeros((SB_HEAD_DIM, TQ), F32)]
    no_earlier = jnp.where(qb > 0, 0.0, -MASKED_NZ).astype(F32)
    tiles = [(qb * n_diag + d, d * TK, True, None) for d in reversed(range(n_diag))]
    tiles += [(jnp.maximum(qb * n_diag - 1 - u, 0), 0, False, no_earlier) for u in range(KEY_TILES_PER_ITER)]
    state = run(tiles, tuple(state))

    n_steps = qb * (n_diag // KEY_TILES_PER_ITER)

    def body(loop_state):
        i, state = loop_state[0], loop_state[2:]
        kb = qb * n_diag - 1 - KEY_TILES_PER_ITER * i
        state = run([(kb - u, 0, False, None) for u in range(KEY_TILES_PER_ITER)], state)
        return (i + 1, jnp.logical_and(i + 1 < n_steps, some_query_alive(state))) + state

    go = jnp.logical_and(1 < n_steps, some_query_alive(state))
    state = lax.while_loop(lambda loop_state: loop_state[1], body, (jnp.int32(1), go) + state)[2:]
    accs = [state[2 * head + 1] for head in range(HEADS_PER_STEP)]
    o_ref[0] = jnp.concatenate(accs, axis=0).T


def _attn(q, k, v):
    nb, seq, _ = q.shape
    n_slabs = SB_WIDTH // SLAB
    return pl.pallas_call(
        _attn_kernel,
        grid=(nb, n_slabs, seq // TQ),
        in_specs=[
            pl.BlockSpec((1, TQ, SLAB), lambda b, s, i: (b, i, s)),
            pl.BlockSpec((1, seq, SLAB), lambda b, s, i: (b, 0, s)),
            pl.BlockSpec((1, seq, SLAB), lambda b, s, i: (b, 0, s)),
        ],
        out_specs=pl.BlockSpec((1, TQ, SLAB), lambda b, s, i: (b, i, s)),
        out_shape=jax.ShapeDtypeStruct((nb, seq, SB_WIDTH), F32),
        scratch_shapes=[pltpu.VMEM((seq // TK, SLAB, TK), BF16)],
        compiler_params=pltpu.CompilerParams(
            dimension_semantics=("arbitrary", "arbitrary", "arbitrary"),
            vmem_limit_bytes=VMEM_LIMIT_BYTES),
        name="sb_attn",
    )(q, k, v)


def _mix_out_kernel(x_ref, ylru_ref, ysb_ref, gsb_ref, wout_ref, g2_ref, wup_ref, wdn_ref, gf_ref, o_ref):
    ysb = _rms(ysb_ref[...], gsb_ref[...]).astype(BF16)
    h = x_ref[...]
    h = h + jnp.dot(ylru_ref[...], wout_ref[0:LRU_WIDTH, :], preferred_element_type=F32)
    h = h + jnp.dot(ysb, wout_ref[LRU_WIDTH:LRU_WIDTH + SB_WIDTH, :], preferred_element_type=F32)
    hn = _rms(h, g2_ref[...]).astype(BF16)
    up = jnp.maximum(jnp.dot(hn, wup_ref[...], preferred_element_type=F32), 0.0)
    h = h + jnp.dot((up * up).astype(BF16), wdn_ref[...], preferred_element_type=F32)
    o_ref[...] = _rms(h, gf_ref[...])


def _mix_out(x2, ylru2, ysb2, gsb, wout, g2, wup, wdn, gf):
    m = x2.shape[0]
    const2 = lambda i: (0, 0)
    row = lambda i: (i, 0)
    resident = dict(pipeline_mode=pl.Buffered(1))
    return pl.pallas_call(
        _mix_out_kernel,
        grid=(m // TM,),
        in_specs=[
            pl.BlockSpec((TM, D_MODEL), row),
            pl.BlockSpec((TM, LRU_WIDTH), row),
            pl.BlockSpec((TM, SB_WIDTH), row),
            pl.BlockSpec((1, SB_WIDTH), const2),
            pl.BlockSpec(wout.shape, const2, **resident),
            pl.BlockSpec((1, D_MODEL), const2),
            pl.BlockSpec(wup.shape, const2, **resident),
            pl.BlockSpec(wdn.shape, const2, **resident),
            pl.BlockSpec((1, D_MODEL), const2),
        ],
        out_specs=pl.BlockSpec((TM, D_MODEL), row),
        out_shape=jax.ShapeDtypeStruct((m, D_MODEL), F32),
        compiler_params=pltpu.CompilerParams(
            dimension_semantics=("arbitrary",), vmem_limit_bytes=VMEM_LIMIT_BYTES),
        name="mix_out",
    )(x2, ylru2, ysb2, gsb, wout, g2, wup, wdn, gf)


def _block_diag(w):
    nblk, n, _ = w.shape
    eye = jnp.eye(nblk, dtype=w.dtype)
    return (eye[:, None, :, None] * w[:, :, None, :]).reshape(nblk * n, nblk * n)


def kernel(x, norm1_g, w_in, conv_w, conv_b, lru_w_a, lru_b_a, lru_w_x, lru_b_x, lru_lambda,
           lru_out_g, sb_out_g, w_out, norm2_g, w_up, w_down, final_g):
    nb, seq, d = x.shape
    assert w_in.shape[0] == 1, "kernel supports depth 1 only"
    layer = 0
    ylru, q, k, v = _mix_in(
        x, norm1_g[layer].reshape(1, d), w_in[layer].astype(BF16),
        conv_w[layer], conv_b[layer].reshape(1, LRU_WIDTH),
        _block_diag(lru_w_a[layer]).astype(BF16), lru_b_a[layer].reshape(1, LRU_WIDTH),
        _block_diag(lru_w_x[layer]).astype(BF16), lru_b_x[layer].reshape(1, LRU_WIDTH),
        lru_lambda[layer].reshape(1, LRU_WIDTH), lru_out_g[layer].reshape(1, LRU_WIDTH))
    ysb = _attn(q, k, v)
    out = _mix_out(
        x.reshape(nb * seq, d), ylru.reshape(nb * seq, LRU_WIDTH), ysb.reshape(nb * seq, SB_WIDTH),
        sb_out_g[layer].reshape(1, SB_WIDTH), w_out[layer].astype(BF16),
        norm2_g[layer].reshape(1, d),
        w_up[layer].astype(BF16), w_down[layer].astype(BF16),
        final_g.reshape(1, d))
    return out.reshape(nb, seq, d)
```

```python
import math

import numpy as np

import jax
import jax.numpy as jnp
from jax import lax
from jax.experimental import pallas as pl
from jax.experimental.pallas import tpu as pltpu

D_MODEL = 1024
LRU_WIDTH = 512
LRU_BLOCKS = 8
LRU_BLOCK = LRU_WIDTH // LRU_BLOCKS
CONV_WIDTH = 4
LRU_C = 8.0
SB_WIDTH = 512
SB_HEADS = 8
SB_HEAD_DIM = SB_WIDTH // SB_HEADS
D_FF = 4 * D_MODEL
EPS = 1e-6

F32 = jnp.float32
BF16 = jnp.bfloat16

LANES = 128
SUBLANES = 8
VMEM_LIMIT_BYTES = 56 * 1024 * 1024

TT = 32
TQ = 256
TK = 128
SLAB = 256
HEADS_PER_STEP = SLAB // SB_HEAD_DIM
HEADS_PER_GROUP = 4
KEY_TILES_PER_ITER = 2
LOG2E = math.log2(math.e)
MASKED_NZ = 1e30
F32_EXP_UNDERFLOW = -105.0
TM = 512


def _rms(x, g):
    ms = jnp.mean(x * x, axis=-1, keepdims=True)
    return x * lax.rsqrt(ms + EPS) * g


def _sigmoid(x):
    return 1.0 / (1.0 + jnp.exp(-x))


def _gelu_tanh(x):
    c = math.sqrt(2.0 / math.pi)
    return 0.5 * x * (1.0 + jnp.tanh(c * (x + 0.044715 * (x * x * x))))


def _mix_in_kernel(x_ref, perm_ref, permt_ref, g1_ref, win_ref, cw_ref, cb_ref, wa_ref, ba_ref, wx_ref, bx_ref,
                   lam_ref, gout_ref,
                   ylru_ref, q_ref, k_ref, v_ref,
                   xl_ext, h_state):
    nb = x_ref.shape[0]
    rows = nb * TT
    grows = perm_ref.shape[0]
    gsz = grows // TT
    n_groups = nb // gsz
    halo = (CONV_WIDTH - 1) * gsz

    @pl.when(pl.program_id(0) == 0)
    def _():
        xl_ext[:, 0:halo, :] = jnp.zeros((n_groups, halo, LRU_WIDTH), F32)
        h_state[...] = jnp.zeros_like(h_state)

    x = x_ref[...].reshape(rows, D_MODEL)
    xn = _rms(x, g1_ref[...]).astype(BF16)

    def proj(lhs, lo, width):
        return jnp.dot(lhs, win_ref[:, lo:lo + width], preferred_element_type=F32)

    def group(m, g):
        return m[g * grows:(g + 1) * grows, :]

    base = 2 * LRU_WIDTH

    xt = jnp.concatenate(
        [jnp.dot(perm_ref[...], group(xn, g), preferred_element_type=F32) for g in range(n_groups)],
        axis=0).astype(BF16)
    xl = proj(xt, 0, LRU_WIDTH)
    gate = proj(xt, LRU_WIDTH, LRU_WIDTH)

    q_ref[...] = (proj(xn, base, SB_WIDTH) * (-1.0 / math.sqrt(SB_HEAD_DIM))).astype(BF16).reshape(nb, TT, SB_WIDTH)

    cs = []
    for g in range(n_groups):
        xl_ext[g, halo:halo + grows, :] = group(xl, g)
        cg = cb_ref[...] + cw_ref[0:1, :] * xl_ext[g, 0:grows, :]
        for kk in range(1, CONV_WIDTH):
            cg = cg + cw_ref[kk:kk + 1, :] * xl_ext[g, kk * gsz:kk * gsz + grows, :]
        xl_ext[g, 0:halo, :] = xl_ext[g, grows:grows + halo, :]
        cs.append(cg)
    c = jnp.concatenate(cs, axis=0)

    cb16 = c.astype(BF16)
    r = _sigmoid(jnp.dot(cb16, wa_ref[...], preferred_element_type=F32) + ba_ref[...])
    i = _sigmoid(jnp.dot(cb16, wx_ref[...], preferred_element_type=F32) + bx_ref[...])

    k_ref[...] = proj(xn, base + SB_WIDTH, SB_WIDTH).astype(BF16).reshape(nb, TT, SB_WIDTH)
    v_ref[...] = proj(xn, base + 2 * SB_WIDTH, SB_WIDTH).astype(BF16).reshape(nb, TT, SB_WIDTH)
    lam = lam_ref[...]
    softplus_neg_lam = jnp.maximum(-lam, 0.0) + jnp.log(1.0 + jnp.exp(-jnp.abs(lam)))
    log_a = (-LRU_C * softplus_neg_lam) * r
    a = jnp.exp(log_a)
    bterm = jnp.sqrt(-jnp.tanh(log_a) * (a * a + 1.0)) * (i * c)

    h = [h_state[g] for g in range(n_groups)]
    hs = [[] for _ in range(n_groups)]
    for t in range(TT):
        for g in range(n_groups):
            r0 = g * grows + t * gsz
            h[g] = a[r0:r0 + gsz, :] * h[g] + bterm[r0:r0 + gsz, :]
            hs[g].append(h[g])
    for g in range(n_groups):
        h_state[g] = h[g]

    y = jnp.concatenate([hg for g in range(n_groups) for hg in hs[g]], axis=0) * _gelu_tanh(gate)
    yn = _rms(y, gout_ref[...]).astype(BF16)
    yb = jnp.concatenate(
        [jnp.dot(permt_ref[...], group(yn, g), preferred_element_type=F32) for g in range(n_groups)], axis=0)
    ylru_ref[...] = yb.astype(BF16).reshape(nb, TT, LRU_WIDTH)


def _time_major_perm(gsz):
    p = np.zeros((gsz * TT, gsz * TT), np.float32)
    for b in range(gsz):
        for t in range(TT):
            p[t * gsz + b, b * TT + t] = 1.0
    return p


def _mix_in(x, g1, win, cw, cb, wa, ba, wx, bx, lam, gout):
    nb, seq, _ = x.shape
    gsz = SUBLANES if nb % SUBLANES == 0 else nb
    grows = gsz * TT
    perm = _time_major_perm(gsz)
    const2 = lambda j: (0, 0)
    out_sds = jax.ShapeDtypeStruct((nb, seq, LRU_WIDTH), BF16)
    out_spec = pl.BlockSpec((nb, TT, LRU_WIDTH), lambda j: (0, j, 0))
    return pl.pallas_call(
        _mix_in_kernel,
        grid=(seq // TT,),
        in_specs=[
            pl.BlockSpec((nb, TT, D_MODEL), lambda j: (0, j, 0)),
            pl.BlockSpec((grows, grows), const2),
            pl.BlockSpec((grows, grows), const2),
            pl.BlockSpec((1, D_MODEL), const2),
            pl.BlockSpec(win.shape, const2),
            pl.BlockSpec(cw.shape, const2),
            pl.BlockSpec((1, LRU_WIDTH), const2),
            pl.BlockSpec(wa.shape, const2),
            pl.BlockSpec((1, LRU_WIDTH), const2),
            pl.BlockSpec(wx.shape, const2),
            pl.BlockSpec((1, LRU_WIDTH), const2),
            pl.BlockSpec((1, LRU_WIDTH), const2),
            pl.BlockSpec((1, LRU_WIDTH), const2),
        ],
        out_specs=[out_spec, out_spec, out_spec, out_spec],
        out_shape=[out_sds, out_sds, out_sds, out_sds],
        scratch_shapes=[
            pltpu.VMEM((nb // gsz, (CONV_WIDTH - 1) * gsz + grows, LRU_WIDTH), F32),
            pltpu.VMEM((nb // gsz, gsz, LRU_WIDTH), F32),
        ],
        compiler_params=pltpu.CompilerParams(
            dimension_semantics=("arbitrary",), vmem_limit_bytes=VMEM_LIMIT_BYTES),
        name="mix_in",
    )(x, jnp.asarray(perm, BF16), jnp.asarray(perm.T, BF16), g1, win, cw, cb, wa, ba, wx, bx, lam, gout)


def _attn_kernel(q_ref, k_ref, v_ref, o_ref, vt_s):
    n_kblocks = vt_s.shape[0]

    for cblk in range(n_kblocks):
        blk = v_ref[0, cblk * TK:(cblk + 1) * TK, :].astype(F32)
        vt_s[cblk] = blk.T.astype(BF16)

    row_i = lax.broadcasted_iota(jnp.int32, (TK, TK), 0)
    col_i = lax.broadcasted_iota(jnp.int32, (TK, TK), 1)
    key_before_query = row_i < col_i
    upper = jnp.where(col_i >= row_i, 1.0, 0.0).astype(BF16)

    def query_tile(qb, unused):
        rows = pl.ds(pl.multiple_of(qb * TQ, TQ), TQ)
        q = q_ref[0, rows, :]
        lane = lax.broadcasted_iota(jnp.int32, (TQ, SLAB), 1)
        qhs = [jnp.where((lane >= h * SB_HEAD_DIM) & (lane < (h + 1) * SB_HEAD_DIM), q, jnp.zeros_like(q))
               for h in range(HEADS_PER_STEP)]

        def run(tiles, state):
            nz, surv, out = {}, {}, {}

            def scores(heads):
                for n, (kb, col0, diagonal, _) in enumerate(tiles):
                    for h in heads:
                        kblk = k_ref[0, pl.ds(pl.multiple_of(kb * TK, TK), TK), :]
                        v = lax.dot_general(kblk, qhs[h][col0:, :], (((1,), (1,)), ((), ())),
                                            preferred_element_type=F32)
                        if diagonal:
                            first = jnp.where(key_before_query, v[:, :TK], MASKED_NZ)
                            v = first if v.shape[1] == TK else jnp.concatenate([first, v[:, TK:]], axis=1)
                        nz[n, h] = v

            def survival(heads):
                for n in range(len(tiles)):
                    for h in heads:
                        e = jnp.exp2(jnp.abs(nz[n, h]) * (-LOG2E))
                        log_1m_beta = jnp.minimum(nz[n, h], 0.0) - jnp.log(1.0 + e)
                        surv[n, h] = jnp.dot(upper, log_1m_beta.astype(BF16), preferred_element_type=F32)

            def weigh(heads):
                for h in heads:
                    carry, acc = state[2 * h], state[2 * h + 1]
                    atts, vts = [], []
                    for n, (kb, col0, _, bias) in enumerate(tiles):
                        seen = carry[:, col0:]
                        seen_b = seen if bias is None else seen + bias
                        att = jnp.exp((surv[n, h] - nz[n, h]) + seen_b).astype(BF16)
                        vt = vt_s[kb, h * SB_HEAD_DIM:(h + 1) * SB_HEAD_DIM, :]
                        if col0 == 0:
                            atts.append(att)
                            vts.append(vt)
                            carry = carry + surv[n, h][0:1, :]
                        else:
                            part = acc[:, col0:] + jnp.dot(vt, att, preferred_element_type=F32)
                            acc = jnp.concatenate([acc[:, :col0], part], axis=1)
                            carry = jnp.concatenate([carry[:, :col0], seen + surv[n, h][0:1, :]], axis=1)
                    acc = acc + jnp.dot(jnp.concatenate(vts, axis=1), jnp.concatenate(atts, axis=0),
                                        preferred_element_type=F32)
                    out[h] = (carry, acc)

            groups = [list(range(g, g + HEADS_PER_GROUP)) for g in range(0, HEADS_PER_STEP, HEADS_PER_GROUP)]
            scores(groups[0])
            survival(groups[0])
            for prev, cur in zip(groups[:-1], groups[1:]):
                scores(cur)
                weigh(prev)
                survival(cur)
            weigh(groups[-1])
            return tuple(x for h in range(HEADS_PER_STEP) for x in out[h])

        def some_query_alive(state):
            top = state[0]
            for head in range(1, HEADS_PER_STEP):
                top = jnp.maximum(top, state[2 * head])
            return jnp.max(top) >= F32_EXP_UNDERFLOW

        n_diag = TQ // TK
        state = []
        for head in range(HEADS_PER_STEP):
            state += [jnp.zeros((1, TQ), F32), jnp.zeros((SB_HEAD_DIM, TQ), F32)]
        no_earlier = jnp.where(qb > 0, 0.0, -MASKED_NZ).astype(F32)
        tiles = [(qb * n_diag + d, d * TK, True, None) for d in reversed(range(n_diag))]
        tiles += [(jnp.maximum(qb * n_diag - 1 - u, 0), 0, False, no_earlier) for u in range(KEY_TILES_PER_ITER)]
        state = run(tiles, tuple(state))

        n_steps = qb * (n_diag // KEY_TILES_PER_ITER)

        def body(loop_state):
            i, state = loop_state[0], loop_state[2:]
            kb = qb * n_diag - 1 - KEY_TILES_PER_ITER * i
            state = run([(kb - u, 0, False, None) for u in range(KEY_TILES_PER_ITER)], state)
            return (i + 1, jnp.logical_and(i + 1 < n_steps, some_query_alive(state))) + state

        go = jnp.logical_and(1 < n_steps, some_query_alive(state))
        state = lax.while_loop(lambda loop_state: loop_state[1], body, (jnp.int32(1), go) + state)[2:]
        accs = [state[2 * head + 1] for head in range(HEADS_PER_STEP)]
        o_ref[0, rows, :] = jnp.concatenate(accs, axis=0).T
        return unused

    lax.fori_loop(0, q_ref.shape[1] // TQ, query_tile, 0)


def _attn(q, k, v):
    nb, seq, _ = q.shape
    n_slabs = SB_WIDTH // SLAB
    slab_spec = pl.BlockSpec((1, seq, SLAB), lambda b, s: (b, 0, s))
    return pl.pallas_call(
        _attn_kernel,
        grid=(nb, n_slabs),
        in_specs=[slab_spec, slab_spec, slab_spec],
        out_specs=slab_spec,
        out_shape=jax.ShapeDtypeStruct((nb, seq, SB_WIDTH), F32),
        scratch_shapes=[pltpu.VMEM((seq // TK, SLAB, TK), BF16)],
        compiler_params=pltpu.CompilerParams(
            dimension_semantics=("arbitrary", "arbitrary"),
            vmem_limit_bytes=VMEM_LIMIT_BYTES),
        name="sb_attn",
    )(q, k, v)


def _mix_out_kernel(x_ref, ylru_ref, ysb_ref, gsb_ref, wout_ref, g2_ref, wup_ref, wdn_ref, gf_ref, o_ref):
    ysb = _rms(ysb_ref[...], gsb_ref[...]).astype(BF16)
    h = x_ref[...]
    h = h + jnp.dot(ylru_ref[...], wout_ref[0:LRU_WIDTH, :], preferred_element_type=F32)
    h = h + jnp.dot(ysb, wout_ref[LRU_WIDTH:LRU_WIDTH + SB_WIDTH, :], preferred_element_type=F32)
    hn = _rms(h, g2_ref[...]).astype(BF16)
    up = jnp.maximum(jnp.dot(hn, wup_ref[...], preferred_element_type=F32), 0.0)
    h = h + jnp.dot((up * up).astype(BF16), wdn_ref[...], preferred_element_type=F32)
    o_ref[...] = _rms(h, gf_ref[...])


def _mix_out(x2, ylru2, ysb2, gsb, wout, g2, wup, wdn, gf):
    m = x2.shape[0]
    const2 = lambda i: (0, 0)
    row = lambda i: (i, 0)
    resident = dict(pipeline_mode=pl.Buffered(1))
    return pl.pallas_call(
        _mix_out_kernel,
        grid=(m // TM,),
        in_specs=[
            pl.BlockSpec((TM, D_MODEL), row),
            pl.BlockSpec((TM, LRU_WIDTH), row),
            pl.BlockSpec((TM, SB_WIDTH), row),
            pl.BlockSpec((1, SB_WIDTH), const2),
            pl.BlockSpec(wout.shape, const2, **resident),
            pl.BlockSpec((1, D_MODEL), const2),
            pl.BlockSpec(wup.shape, const2, **resident),
            pl.BlockSpec(wdn.shape, const2, **resident),
            pl.BlockSpec((1, D_MODEL), const2),
        ],
        out_specs=pl.BlockSpec((TM, D_MODEL), row),
        out_shape=jax.ShapeDtypeStruct((m, D_MODEL), F32),
        compiler_params=pltpu.CompilerParams(
            dimension_semantics=("arbitrary",), vmem_limit_bytes=VMEM_LIMIT_BYTES),
        name="mix_out",
    )(x2, ylru2, ysb2, gsb, wout, g2, wup, wdn, gf)


def _block_diag(w):
    nblk, n, _ = w.shape
    eye = jnp.eye(nblk, dtype=w.dtype)
    return (eye[:, None, :, None] * w[:, :, None, :]).reshape(nblk * n, nblk * n)


def kernel(x, norm1_g, w_in, conv_w, conv_b, lru_w_a, lru_b_a, lru_w_x, lru_b_x, lru_lambda,
           lru_out_g, sb_out_g, w_out, norm2_g, w_up, w_down, final_g):
    nb, seq, d = x.shape
    assert w_in.shape[0] == 1, "kernel supports depth 1 only"
    layer = 0
    ylru, q, k, v = _mix_in(
        x, norm1_g[layer].reshape(1, d), w_in[layer].astype(BF16),
        conv_w[layer], conv_b[layer].reshape(1, LRU_WIDTH),
        _block_diag(lru_w_a[layer]).astype(BF16), lru_b_a[layer].reshape(1, LRU_WIDTH),
        _block_diag(lru_w_x[layer]).astype(BF16), lru_b_x[layer].reshape(1, LRU_WIDTH),
        lru_lambda[layer].reshape(1, LRU_WIDTH), lru_out_g[layer].reshape(1, LRU_WIDTH))
    ysb = _attn(q, k, v)
    out = _mix_out(
        x.reshape(nb * seq, d), ylru.reshape(nb * seq, LRU_WIDTH), ysb.reshape(nb * seq, SB_WIDTH),
        sb_out_g[layer].reshape(1, SB_WIDTH), w_out[layer].astype(BF16),
        norm2_g[layer].reshape(1, d),
        w_up[layer].astype(BF16), w_down[layer].astype(BF16),
        final_g.reshape(1, d))
    return out.reshape(nb, seq, d)
```

```python
import math

import numpy as np

import jax
import jax.numpy as jnp
from jax import lax
from jax.experimental import pallas as pl
from jax.experimental.pallas import tpu as pltpu

D_MODEL = 1024
LRU_WIDTH = 512
LRU_BLOCKS = 8
LRU_BLOCK = LRU_WIDTH // LRU_BLOCKS
CONV_WIDTH = 4
LRU_C = 8.0
SB_WIDTH = 512
SB_HEADS = 8
SB_HEAD_DIM = SB_WIDTH // SB_HEADS
D_FF = 4 * D_MODEL
EPS = 1e-6

F32 = jnp.float32
BF16 = jnp.bfloat16

LANES = 128
SUBLANES = 8
VMEM_LIMIT_BYTES = 56 * 1024 * 1024

TT = 32
TQ = 256
TK = 128
SLAB = 256
HEADS_PER_STEP = SLAB // SB_HEAD_DIM
HEADS_PER_GROUP = 2
KEY_TILES_PER_ITER = 2
LOG2E = math.log2(math.e)
MASKED_NZ = 1e30
F32_EXP_UNDERFLOW = -105.0
TM = 512


def _rms(x, g):
    ms = jnp.mean(x * x, axis=-1, keepdims=True)
    return x * lax.rsqrt(ms + EPS) * g


def _sigmoid(x):
    return 0.5 * jnp.tanh(0.5 * x) + 0.5


def _gelu_tanh(x):
    c = math.sqrt(2.0 / math.pi)
    return 0.5 * x * (1.0 + jnp.tanh(c * (x + 0.044715 * (x * x * x))))


def _mix_in_kernel(x_ref, perm_ref, permt_ref, g1_ref, win_ref, cw_ref, cb_ref, wa_ref, ba_ref, wx_ref, bx_ref,
                   lam_ref, gout_ref,
                   ylru_ref, q_ref, k_ref, v_ref,
                   xl_ext, h_state):
    nb = x_ref.shape[0]
    rows = nb * TT
    grows = perm_ref.shape[0]
    gsz = grows // TT
    n_groups = nb // gsz
    halo = (CONV_WIDTH - 1) * gsz

    @pl.when(pl.program_id(0) == 0)
    def _():
        xl_ext[:, 0:halo, :] = jnp.zeros((n_groups, halo, LRU_WIDTH), F32)
        h_state[...] = jnp.zeros_like(h_state)

    x = x_ref[...].reshape(rows, D_MODEL)
    xn = _rms(x, g1_ref[...]).astype(BF16)

    def proj(lhs, lo, width):
        return jnp.dot(lhs, win_ref[:, lo:lo + width], preferred_element_type=F32)

    def group(m, g):
        return m[g * grows:(g + 1) * grows, :]

    base = 2 * LRU_WIDTH

    xt = jnp.concatenate(
        [jnp.dot(perm_ref[...], group(xn, g), preferred_element_type=F32) for g in range(n_groups)],
        axis=0).astype(BF16)
    xl = proj(xt, 0, LRU_WIDTH)
    gate = proj(xt, LRU_WIDTH, LRU_WIDTH)

    q_ref[...] = (proj(xn, base, SB_WIDTH) * (-1.0 / math.sqrt(SB_HEAD_DIM))).astype(BF16).reshape(nb, TT, SB_WIDTH)

    cs = []
    for g in range(n_groups):
        xl_ext[g, halo:halo + grows, :] = group(xl, g)
        cg = cb_ref[...] + cw_ref[0:1, :] * xl_ext[g, 0:grows, :]
        for kk in range(1, CONV_WIDTH):
            cg = cg + cw_ref[kk:kk + 1, :] * xl_ext[g, kk * gsz:kk * gsz + grows, :]
        xl_ext[g, 0:halo, :] = xl_ext[g, grows:grows + halo, :]
        cs.append(cg)
    c = jnp.concatenate(cs, axis=0)

    cb16 = c.astype(BF16)
    r = _sigmoid(jnp.dot(cb16, wa_ref[...], preferred_element_type=F32) + ba_ref[...])
    i = _sigmoid(jnp.dot(cb16, wx_ref[...], preferred_element_type=F32) + bx_ref[...])

    k_ref[...] = proj(xn, base + SB_WIDTH, SB_WIDTH).astype(BF16).reshape(nb, TT, SB_WIDTH)
    v_ref[...] = proj(xn, base + 2 * SB_WIDTH, SB_WIDTH).astype(BF16).reshape(nb, TT, SB_WIDTH)
    lam = lam_ref[...]
    softplus_neg_lam = jnp.maximum(-lam, 0.0) + jnp.log(1.0 + jnp.exp(-jnp.abs(lam)))
    log_a = (-LRU_C * softplus_neg_lam) * r
    a = jnp.exp(log_a)
    bterm = jnp.sqrt(-jnp.tanh(log_a) * (a * a + 1.0)) * (i * c)

    h = [h_state[g] for g in range(n_groups)]
    hs = [[] for _ in range(n_groups)]
    for t in range(TT):
        for g in range(n_groups):
            r0 = g * grows + t * gsz
            h[g] = a[r0:r0 + gsz, :] * h[g] + bterm[r0:r0 + gsz, :]
            hs[g].append(h[g])
    for g in range(n_groups):
        h_state[g] = h[g]

    y = jnp.concatenate([hg for g in range(n_groups) for hg in hs[g]], axis=0) * _gelu_tanh(gate)
    yn = _rms(y, gout_ref[...]).astype(BF16)
    yb = jnp.concatenate(
        [jnp.dot(permt_ref[...], group(yn, g), preferred_element_type=F32) for g in range(n_groups)], axis=0)
    ylru_ref[...] = yb.astype(BF16).reshape(nb, TT, LRU_WIDTH)


def _time_major_perm(gsz):
    p = np.zeros((gsz * TT, gsz * TT), np.float32)
    for b in range(gsz):
        for t in range(TT):
            p[t * gsz + b, b * TT + t] = 1.0
    return p


def _mix_in(x, g1, win, cw, cb, wa, ba, wx, bx, lam, gout):
    nb, seq, _ = x.shape
    gsz = SUBLANES if nb % SUBLANES == 0 else nb
    grows = gsz * TT
    perm = _time_major_perm(gsz)
    const2 = lambda j: (0, 0)
    out_sds = jax.ShapeDtypeStruct((nb, seq, LRU_WIDTH), BF16)
    out_spec = pl.BlockSpec((nb, TT, LRU_WIDTH), lambda j: (0, j, 0))
    return pl.pallas_call(
        _mix_in_kernel,
        grid=(seq // TT,),
        in_specs=[
            pl.BlockSpec((nb, TT, D_MODEL), lambda j: (0, j, 0)),
            pl.BlockSpec((grows, grows), const2),
            pl.BlockSpec((grows, grows), const2),
            pl.BlockSpec((1, D_MODEL), const2),
            pl.BlockSpec(win.shape, const2),
            pl.BlockSpec(cw.shape, const2),
            pl.BlockSpec((1, LRU_WIDTH), const2),
            pl.BlockSpec(wa.shape, const2),
            pl.BlockSpec((1, LRU_WIDTH), const2),
            pl.BlockSpec(wx.shape, const2),
            pl.BlockSpec((1, LRU_WIDTH), const2),
            pl.BlockSpec((1, LRU_WIDTH), const2),
            pl.BlockSpec((1, LRU_WIDTH), const2),
        ],
        out_specs=[out_spec, out_spec, out_spec, out_spec],
        out_shape=[out_sds, out_sds, out_sds, out_sds],
        scratch_shapes=[
            pltpu.VMEM((nb // gsz, (CONV_WIDTH - 1) * gsz + grows, LRU_WIDTH), F32),
            pltpu.VMEM((nb // gsz, gsz, LRU_WIDTH), F32),
        ],
        compiler_params=pltpu.CompilerParams(
            dimension_semantics=("arbitrary",), vmem_limit_bytes=VMEM_LIMIT_BYTES),
        name="mix_in",
    )(x, jnp.asarray(perm, BF16), jnp.asarray(perm.T, BF16), g1, win, cw, cb, wa, ba, wx, bx, lam, gout)


def _attn_kernel(q_ref, k_ref, v_ref, o_ref, vt_s):
    n_kblocks = vt_s.shape[0]

    for cblk in range(n_kblocks):
        blk = v_ref[0, cblk * TK:(cblk + 1) * TK, :].astype(F32)
        vt_s[cblk] = blk.T.astype(BF16)

    row_i = lax.broadcasted_iota(jnp.int32, (TK, TK), 0)
    col_i = lax.broadcasted_iota(jnp.int32, (TK, TK), 1)
    key_before_query = row_i < col_i
    upper = jnp.where(col_i >= row_i, 1.0, 0.0).astype(BF16)

    def query_tile(qb, unused):
        rows = pl.ds(pl.multiple_of(qb * TQ, TQ), TQ)
        q = q_ref[0, rows, :]
        lane = lax.broadcasted_iota(jnp.int32, (TQ, SLAB), 1)
        qhs = [jnp.where((lane >= h * SB_HEAD_DIM) & (lane < (h + 1) * SB_HEAD_DIM), q, jnp.zeros_like(q))
               for h in range(HEADS_PER_STEP)]

        def run(tiles, state):
            nz, surv, out = {}, {}, {}

            def scores(heads):
                for n, (kb, col0, diagonal, _) in enumerate(tiles):
                    for h in heads:
                        kblk = k_ref[0, pl.ds(pl.multiple_of(kb * TK, TK), TK), :]
                        v = lax.dot_general(kblk, qhs[h][col0:, :], (((1,), (1,)), ((), ())),
                                            preferred_element_type=F32)
                        if diagonal:
                            first = jnp.where(key_before_query, v[:, :TK], MASKED_NZ)
                            v = first if v.shape[1] == TK else jnp.concatenate([first, v[:, TK:]], axis=1)
                        nz[n, h] = v

            def survival(heads):
                for n in range(len(tiles)):
                    for h in heads:
                        e = jnp.exp2(jnp.abs(nz[n, h]) * (-LOG2E))
                        log_1m_beta = jnp.minimum(nz[n, h], 0.0) - jnp.log(1.0 + e)
                        surv[n, h] = jnp.dot(upper, log_1m_beta.astype(BF16), preferred_element_type=F32)

            def weigh(heads):
                for h in heads:
                    carry, acc = state[2 * h], state[2 * h + 1]
                    atts, vts = [], []
                    for n, (kb, col0, _, bias) in enumerate(tiles):
                        seen = carry[:, col0:]
                        seen_b = seen if bias is None else seen + bias
                        att = jnp.exp((surv[n, h] - nz[n, h]) + seen_b).astype(BF16)
                        vt = vt_s[kb, h * SB_HEAD_DIM:(h + 1) * SB_HEAD_DIM, :]
                        if col0 == 0:
                            atts.append(att)
                            vts.append(vt)
                            carry = carry + surv[n, h][0:1, :]
                        else:
                            part = acc[:, col0:] + jnp.dot(vt, att, preferred_element_type=F32)
                            acc = jnp.concatenate([acc[:, :col0], part], axis=1)
                            carry = jnp.concatenate([carry[:, :col0], seen + surv[n, h][0:1, :]], axis=1)
                    acc = acc + jnp.dot(jnp.concatenate(vts, axis=1), jnp.concatenate(atts, axis=0),
                                        preferred_element_type=F32)
                    out[h] = (carry, acc)

            groups = [list(range(g, g + HEADS_PER_GROUP)) for g in range(0, HEADS_PER_STEP, HEADS_PER_GROUP)]
            scores(groups[0])
            survival(groups[0])
            for prev, cur in zip(groups[:-1], groups[1:]):
                scores(cur)
                weigh(prev)
                survival(cur)
            weigh(groups[-1])
            return tuple(x for h in range(HEADS_PER_STEP) for x in out[h])

        def some_query_alive(state):
            top = state[0]
            for head in range(1, HEADS_PER_STEP):
                top = jnp.maximum(top, state[2 * head])
            return jnp.max(top) >= F32_EXP_UNDERFLOW

        n_diag = TQ // TK
        state = []
        for head in range(HEADS_PER_STEP):
            state += [jnp.zeros((1, TQ), F32), jnp.zeros((SB_HEAD_DIM, TQ), F32)]
        no_earlier = jnp.where(qb > 0, 0.0, -MASKED_NZ).astype(F32)
        tiles = [(qb * n_diag + d, d * TK, True, None) for d in reversed(range(n_diag))]
        tiles += [(jnp.maximum(qb * n_diag - 1 - u, 0), 0, False, no_earlier) for u in range(KEY_TILES_PER_ITER)]
        state = run(tiles, tuple(state))

        n_steps = qb * (n_diag // KEY_TILES_PER_ITER)

        def body(loop_state):
            i, state = loop_state[0], loop_state[2:]
            kb = qb * n_diag - 1 - KEY_TILES_PER_ITER * i
            state = run([(kb - u, 0, False, None) for u in range(KEY_TILES_PER_ITER)], state)
            return (i + 1, jnp.logical_and(i + 1 < n_steps, some_query_alive(state))) + state

        go = jnp.logical_and(1 < n_steps, some_query_alive(state))
        state = lax.while_loop(lambda loop_state: loop_state[1], body, (jnp.int32(1), go) + state)[2:]
        accs = [state[2 * head + 1] for head in range(HEADS_PER_STEP)]
        o_ref[0, rows, :] = jnp.concatenate(accs, axis=0).T
        return unused

    lax.fori_loop(0, q_ref.shape[1] // TQ, query_tile, 0)


def _attn(q, k, v):
    nb, seq, _ = q.shape
    n_slabs = SB_WIDTH // SLAB
    slab_spec = pl.BlockSpec((1, seq, SLAB), lambda b, s: (b, 0, s))
    return pl.pallas_call(
        _attn_kernel,
        grid=(nb, n_slabs),
        in_specs=[slab_spec, slab_spec, slab_spec],
        out_specs=slab_spec,
        out_shape=jax.ShapeDtypeStruct((nb, seq, SB_WIDTH), F32),
        scratch_shapes=[pltpu.VMEM((seq // TK, SLAB, TK), BF16)],
        compiler_params=pltpu.CompilerParams(
            dimension_semantics=("arbitrary", "arbitrary"),
            vmem_limit_bytes=VMEM_LIMIT_BYTES),
        name="sb_attn",
    )(q, k, v)


def _mix_out_kernel(x_ref, ylru_ref, ysb_ref, gsb_ref, wout_ref, g2_ref, wup_ref, wdn_ref, gf_ref, o_ref):
    ysb = _rms(ysb_ref[...], gsb_ref[...]).astype(BF16)
    h = x_ref[...]
    h = h + jnp.dot(ylru_ref[...], wout_ref[0:LRU_WIDTH, :], preferred_element_type=F32)
    h = h + jnp.dot(ysb, wout_ref[LRU_WIDTH:LRU_WIDTH + SB_WIDTH, :], preferred_element_type=F32)
    hn = _rms(h, g2_ref[...]).astype(BF16)
    up = jnp.maximum(jnp.dot(hn, wup_ref[...], preferred_element_type=F32), 0.0)
    h = h + jnp.dot((up * up).astype(BF16), wdn_ref[...], preferred_element_type=F32)
    o_ref[...] = _rms(h, gf_ref[...])


def _mix_out(x2, ylru2, ysb2, gsb, wout, g2, wup, wdn, gf):
    m = x2.shape[0]
    const2 = lambda i: (0, 0)
    row = lambda i: (i, 0)
    resident = dict(pipeline_mode=pl.Buffered(1))
    return pl.pallas_call(
        _mix_out_kernel,
        grid=(m // TM,),
        in_specs=[
            pl.BlockSpec((TM, D_MODEL), row),
            pl.BlockSpec((TM, LRU_WIDTH), row),
            pl.BlockSpec((TM, SB_WIDTH), row),
            pl.BlockSpec((1, SB_WIDTH), const2),
            pl.BlockSpec(wout.shape, const2, **resident),
            pl.BlockSpec((1, D_MODEL), const2),
            pl.BlockSpec(wup.shape, const2, **resident),
            pl.BlockSpec(wdn.shape, const2, **resident),
            pl.BlockSpec((1, D_MODEL), const2),
        ],
        out_specs=pl.BlockSpec((TM, D_MODEL), row),
        out_shape=jax.ShapeDtypeStruct((m, D_MODEL), F32),
        compiler_params=pltpu.CompilerParams(
            dimension_semantics=("arbitrary",), vmem_limit_bytes=VMEM_LIMIT_BYTES),
        name="mix_out",
    )(x2, ylru2, ysb2, gsb, wout, g2, wup, wdn, gf)


def _block_diag(w):
    nblk, n, _ = w.shape
    eye = jnp.eye(nblk, dtype=w.dtype)
    return (eye[:, None, :, None] * w[:, :, None, :]).reshape(nblk * n, nblk * n)


def kernel(x, norm1_g, w_in, conv_w, conv_b, lru_w_a, lru_b_a, lru_w_x, lru_b_x, lru_lambda,
           lru_out_g, sb_out_g, w_out, norm2_g, w_up, w_down, final_g):
    nb, seq, d = x.shape
    assert w_in.shape[0] == 1, "kernel supports depth 1 only"
    layer = 0
    ylru, q, k, v = _mix_in(
        x, norm1_g[layer].reshape(1, d), w_in[layer].astype(BF16),
        conv_w[layer], conv_b[layer].reshape(1, LRU_WIDTH),
        _block_diag(lru_w_a[layer]).astype(BF16), lru_b_a[layer].reshape(1, LRU_WIDTH),
        _block_diag(lru_w_x[layer]).astype(BF16), lru_b_x[layer].reshape(1, LRU_WIDTH),
        lru_lambda[layer].reshape(1, LRU_WIDTH), lru_out_g[layer].reshape(1, LRU_WIDTH))
    ysb = _attn(q, k, v)
    out = _mix_out(
        x.reshape(nb * seq, d), ylru.reshape(nb * seq, LRU_WIDTH), ysb.reshape(nb * seq, SB_WIDTH),
        sb_out_g[layer].reshape(1, SB_WIDTH), w_out[layer].astype(BF16),
        norm2_g[layer].reshape(1, d),
        w_up[layer].astype(BF16), w_down[layer].astype(BF16),
        final_g.reshape(1, d))
    return out.reshape(nb, seq, d)
```

```python
import math

import numpy as np

import jax
import jax.numpy as jnp
from jax import lax
from jax.experimental import pallas as pl
from jax.experimental.pallas import tpu as pltpu

D_MODEL = 1024
LRU_WIDTH = 512
LRU_BLOCKS = 8
LRU_BLOCK = LRU_WIDTH // LRU_BLOCKS
CONV_WIDTH = 4
LRU_C = 8.0
SB_WIDTH = 512
SB_HEADS = 8
SB_HEAD_DIM = SB_WIDTH // SB_HEADS
D_FF = 4 * D_MODEL
EPS = 1e-6

F32 = jnp.float32
BF16 = jnp.bfloat16

LANES = 128
SUBLANES = 8
VMEM_LIMIT_BYTES = 56 * 1024 * 1024

TT = 32
TQ = 256
TK = 128
SLAB = 256
HEADS_PER_STEP = SLAB // SB_HEAD_DIM
HEADS_PER_GROUP = 4
KEY_TILES_PER_ITER = 2
LOG2E = math.log2(math.e)
MASKED_NZ = 1e30
F32_EXP_UNDERFLOW = -105.0
TM = 512


def _rms(x, g):
    ms = jnp.mean(x * x, axis=-1, keepdims=True)
    return x * lax.rsqrt(ms + EPS) * g


def _sigmoid(x):
    return 0.5 * jnp.tanh(0.5 * x) + 0.5


def _gelu_tanh(x):
    c = math.sqrt(2.0 / math.pi)
    return 0.5 * x * (1.0 + jnp.tanh(c * (x + 0.044715 * (x * x * x))))


def _mix_in_kernel(x_ref, perm_ref, permt_ref, g1_ref, win_ref, cw_ref, cb_ref, wa_ref, ba_ref, wx_ref, bx_ref,
                   lam_ref, gout_ref,
                   ylru_ref, q_ref, k_ref, v_ref,
                   xl_ext, h_state):
    nb = x_ref.shape[0]
    rows = nb * TT
    grows = perm_ref.shape[0]
    gsz = grows // TT
    n_groups = nb // gsz
    halo = (CONV_WIDTH - 1) * gsz

    @pl.when(pl.program_id(0) == 0)
    def _():
        xl_ext[:, 0:halo, :] = jnp.zeros((n_groups, halo, LRU_WIDTH), F32)
        h_state[...] = jnp.zeros_like(h_state)

    x = x_ref[...].reshape(rows, D_MODEL)
    xn = _rms(x, g1_ref[...]).astype(BF16)

    def proj(lhs, lo, width):
        return jnp.dot(lhs, win_ref[:, lo:lo + width], preferred_element_type=F32)

    def group(m, g):
        return m[g * grows:(g + 1) * grows, :]

    base = 2 * LRU_WIDTH

    xt = jnp.concatenate(
        [jnp.dot(perm_ref[...], group(xn, g), preferred_element_type=F32) for g in range(n_groups)],
        axis=0).astype(BF16)
    xl = proj(xt, 0, LRU_WIDTH)
    gate = proj(xt, LRU_WIDTH, LRU_WIDTH)

    q_ref[...] = (proj(xn, base, SB_WIDTH) * (-1.0 / math.sqrt(SB_HEAD_DIM))).astype(BF16).reshape(nb, TT, SB_WIDTH)

    cs = []
    for g in range(n_groups):
        xl_ext[g, halo:halo + grows, :] = group(xl, g)
        cg = cb_ref[...] + cw_ref[0:1, :] * xl_ext[g, 0:grows, :]
        for kk in range(1, CONV_WIDTH):
            cg = cg + cw_ref[kk:kk + 1, :] * xl_ext[g, kk * gsz:kk * gsz + grows, :]
        xl_ext[g, 0:halo, :] = xl_ext[g, grows:grows + halo, :]
        cs.append(cg)
    c = jnp.concatenate(cs, axis=0)

    cb16 = c.astype(BF16)
    r = _sigmoid(jnp.dot(cb16, wa_ref[...], preferred_element_type=F32) + ba_ref[...])
    i = _sigmoid(jnp.dot(cb16, wx_ref[...], preferred_element_type=F32) + bx_ref[...])

    k_ref[...] = proj(xn, base + SB_WIDTH, SB_WIDTH).astype(BF16).reshape(nb, TT, SB_WIDTH)
    v_ref[...] = proj(xn, base + 2 * SB_WIDTH, SB_WIDTH).astype(BF16).reshape(nb, TT, SB_WIDTH)
    lam = lam_ref[...]
    softplus_neg_lam = jnp.maximum(-lam, 0.0) + jnp.log(1.0 + jnp.exp(-jnp.abs(lam)))
    log_a = (-LRU_C * softplus_neg_lam) * r
    a = jnp.exp(log_a)
    bterm = jnp.sqrt(-jnp.tanh(log_a) * (a * a + 1.0)) * (i * c)

    h = [h_state[g] for g in range(n_groups)]
    hs = [[] for _ in range(n_groups)]
    for t in range(TT):
        for g in range(n_groups):
            r0 = g * grows + t * gsz
            h[g] = a[r0:r0 + gsz, :] * h[g] + bterm[r0:r0 + gsz, :]
            hs[g].append(h[g])
    for g in range(n_groups):
        h_state[g] = h[g]

    y = jnp.concatenate([hg for g in range(n_groups) for hg in hs[g]], axis=0) * _gelu_tanh(gate)
    yn = _rms(y, gout_ref[...]).astype(BF16)
    yb = jnp.concatenate(
        [jnp.dot(permt_ref[...], group(yn, g), preferred_element_type=F32) for g in range(n_groups)], axis=0)
    ylru_ref[...] = yb.astype(BF16).reshape(nb, TT, LRU_WIDTH)


def _time_major_perm(gsz):
    p = np.zeros((gsz * TT, gsz * TT), np.float32)
    for b in range(gsz):
        for t in range(TT):
            p[t * gsz + b, b * TT + t] = 1.0
    return p


def _mix_in(x, g1, win, cw, cb, wa, ba, wx, bx, lam, gout):
    nb, seq, _ = x.shape
    gsz = SUBLANES if nb % SUBLANES == 0 else nb
    grows = gsz * TT
    perm = _time_major_perm(gsz)
    const2 = lambda j: (0, 0)
    out_sds = jax.ShapeDtypeStruct((nb, seq, LRU_WIDTH), BF16)
    out_spec = pl.BlockSpec((nb, TT, LRU_WIDTH), lambda j: (0, j, 0))
    return pl.pallas_call(
        _mix_in_kernel,
        grid=(seq // TT,),
        in_specs=[
            pl.BlockSpec((nb, TT, D_MODEL), lambda j: (0, j, 0)),
            pl.BlockSpec((grows, grows), const2),
            pl.BlockSpec((grows, grows), const2),
            pl.BlockSpec((1, D_MODEL), const2),
            pl.BlockSpec(win.shape, const2),
            pl.BlockSpec(cw.shape, const2),
            pl.BlockSpec((1, LRU_WIDTH), const2),
            pl.BlockSpec(wa.shape, const2),
            pl.BlockSpec((1, LRU_WIDTH), const2),
            pl.BlockSpec(wx.shape, const2),
            pl.BlockSpec((1, LRU_WIDTH), const2),
            pl.BlockSpec((1, LRU_WIDTH), const2),
            pl.BlockSpec((1, LRU_WIDTH), const2),
        ],
        out_specs=[out_spec, out_spec, out_spec, out_spec],
        out_shape=[out_sds, out_sds, out_sds, out_sds],
        scratch_shapes=[
            pltpu.VMEM((nb // gsz, (CONV_WIDTH - 1) * gsz + grows, LRU_WIDTH), F32),
            pltpu.VMEM((nb // gsz, gsz, LRU_WIDTH), F32),
        ],
        compiler_params=pltpu.CompilerParams(
            dimension_semantics=("arbitrary",), vmem_limit_bytes=VMEM_LIMIT_BYTES),
        name="mix_in",
    )(x, jnp.asarray(perm, BF16), jnp.asarray(perm.T, BF16), g1, win, cw, cb, wa, ba, wx, bx, lam, gout)


def _attn_kernel(q_ref, k_ref, v_ref, o_ref, vt_s):
    n_kblocks = vt_s.shape[0]

    for cblk in range(n_kblocks):
        blk = v_ref[0, cblk * TK:(cblk + 1) * TK, :].astype(F32)
        vt_s[cblk] = blk.T.astype(BF16)

    row_i = lax.broadcasted_iota(jnp.int32, (TK, TK), 0)
    col_i = lax.broadcasted_iota(jnp.int32, (TK, TK), 1)
    key_before_query = row_i < col_i
    upper = jnp.where(col_i >= row_i, 1.0, 0.0).astype(BF16)

    def query_tile(qb, unused):
        rows = pl.ds(pl.multiple_of(qb * TQ, TQ), TQ)
        q = q_ref[0, rows, :]
        lane = lax.broadcasted_iota(jnp.int32, (TQ, SLAB), 1)
        qhs = [jnp.where((lane >= h * SB_HEAD_DIM) & (lane < (h + 1) * SB_HEAD_DIM), q, jnp.zeros_like(q))
               for h in range(HEADS_PER_STEP)]

        def run(tiles, state):
            nz, surv, out = {}, {}, {}

            def scores(heads):
                for n, (kb, col0, diagonal, _) in enumerate(tiles):
                    for h in heads:
                        kblk = k_ref[0, pl.ds(pl.multiple_of(kb * TK, TK), TK), :]
                        v = lax.dot_general(kblk, qhs[h][col0:, :], (((1,), (1,)), ((), ())),
                                            preferred_element_type=F32)
                        if diagonal:
                            first = jnp.where(key_before_query, v[:, :TK], MASKED_NZ)
                            v = first if v.shape[1] == TK else jnp.concatenate([first, v[:, TK:]], axis=1)
                        nz[n, h] = v

            def survival(heads):
                for n in range(len(tiles)):
                    for h in heads:
                        e = jnp.exp2(jnp.abs(nz[n, h]) * (-LOG2E))
                        log_1m_beta = jnp.minimum(nz[n, h], 0.0) - jnp.log(1.0 + e)
                        surv[n, h] = jnp.dot(upper, log_1m_beta.astype(BF16), preferred_element_type=F32)

            def weigh(heads):
                for h in heads:
                    carry, acc = state[2 * h], state[2 * h + 1]
                    atts, vts = [], []
                    for n, (kb, col0, _, bias) in enumerate(tiles):
                        seen = carry[:, col0:]
                        seen_b = seen if bias is None else seen + bias
                        att = jnp.exp((surv[n, h] - nz[n, h]) + seen_b).astype(BF16)
                        vt = vt_s[kb, h * SB_HEAD_DIM:(h + 1) * SB_HEAD_DIM, :]
                        if col0 == 0:
                            atts.append(att)
                            vts.append(vt)
                            carry = carry + surv[n, h][0:1, :]
                        else:
                            part = acc[:, col0:] + jnp.dot(vt, att, preferred_element_type=F32)
                            acc = jnp.concatenate([acc[:, :col0], part], axis=1)
                            carry = jnp.concatenate([carry[:, :col0], seen + surv[n, h][0:1, :]], axis=1)
                    acc = acc + jnp.dot(jnp.concatenate(vts, axis=1), jnp.concatenate(atts, axis=0),
                                        preferred_element_type=F32)
                    out[h] = (carry, acc)

            groups = [list(range(g, g + HEADS_PER_GROUP)) for g in range(0, HEADS_PER_STEP, HEADS_PER_GROUP)]
            scores(groups[0])
            survival(groups[0])
            for prev, cur in zip(groups[:-1], groups[1:]):
                scores(cur)
                weigh(prev)
                survival(cur)
            weigh(groups[-1])
            return tuple(x for h in range(HEADS_PER_STEP) for x in out[h])

        def some_query_alive(state):
            top = state[0]
            for head in range(1, HEADS_PER_STEP):
                top = jnp.maximum(top, state[2 * head])
            return jnp.max(top) >= F32_EXP_UNDERFLOW

        n_diag = TQ // TK
        state = []
        for head in range(HEADS_PER_STEP):
            state += [jnp.zeros((1, TQ), F32), jnp.zeros((SB_HEAD_DIM, TQ), F32)]
        no_earlier = jnp.where(qb > 0, 0.0, -MASKED_NZ).astype(F32)
        tiles = [(qb * n_diag + d, d * TK, True, None) for d in reversed(range(n_diag))]
        tiles += [(jnp.maximum(qb * n_diag - 1 - u, 0), 0, False, no_earlier) for u in range(KEY_TILES_PER_ITER)]
        state = run(tiles, tuple(state))

        n_steps = qb * (n_diag // KEY_TILES_PER_ITER)

        def body(loop_state):
            i, state = loop_state[0], loop_state[2:]
            kb = qb * n_diag - 1 - KEY_TILES_PER_ITER * i
            state = run([(kb - u, 0, False, None) for u in range(KEY_TILES_PER_ITER)], state)
            return (i + 1, jnp.logical_and(i + 1 < n_steps, some_query_alive(state))) + state

        go = jnp.logical_and(1 < n_steps, some_query_alive(state))
        state = lax.while_loop(lambda loop_state: loop_state[1], body, (jnp.int32(1), go) + state)[2:]
        accs = [state[2 * head + 1] for head in range(HEADS_PER_STEP)]
        o_ref[0, rows, :] = jnp.concatenate(accs, axis=0).T
        return unused

    lax.fori_loop(0, q_ref.shape[1] // TQ, query_tile, 0)


def _attn(q, k, v):
    nb, seq, _ = q.shape
    n_slabs = SB_WIDTH // SLAB
    slab_spec = pl.BlockSpec((1, seq, SLAB), lambda b, s: (b, 0, s))
    return pl.pallas_call(
        _attn_kernel,
        grid=(nb, n_slabs),
        in_specs=[slab_spec, slab_spec, slab_spec],
        out_specs=slab_spec,
        out_shape=jax.ShapeDtypeStruct((nb, seq, SB_WIDTH), F32),
        scratch_shapes=[pltpu.VMEM((seq // TK, SLAB, TK), BF16)],
        compiler_params=pltpu.CompilerParams(
            dimension_semantics=("arbitrary", "arbitrary"),
            vmem_limit_bytes=VMEM_LIMIT_BYTES),
        name="sb_attn",
    )(q, k, v)


def _mix_out_kernel(x_ref, ylru_ref, ysb_ref, gsb_ref, wout_ref, g2_ref, wup_ref, wdn_ref, gf_ref, o_ref):
    ysb = _rms(ysb_ref[...], gsb_ref[...]).astype(BF16)
    h = x_ref[...]
    h = h + jnp.dot(ylru_ref[...], wout_ref[0:LRU_WIDTH, :], preferred_element_type=F32)
    h = h + jnp.dot(ysb, wout_ref[LRU_WIDTH:LRU_WIDTH + SB_WIDTH, :], preferred_element_type=F32)
    hn = _rms(h, g2_ref[...]).astype(BF16)
    up = jnp.maximum(jnp.dot(hn, wup_ref[...], preferred_element_type=F32), 0.0)
    h = h + jnp.dot((up * up).astype(BF16), wdn_ref[...], preferred_element_type=F32)
    o_ref[...] = _rms(h, gf_ref[...])


def _mix_out(x2, ylru2, ysb2, gsb, wout, g2, wup, wdn, gf):
    m = x2.shape[0]
    const2 = lambda i: (0, 0)
    row = lambda i: (i, 0)
    resident = dict(pipeline_mode=pl.Buffered(1))
    return pl.pallas_call(
        _mix_out_kernel,
        grid=(m // TM,),
        in_specs=[
            pl.BlockSpec((TM, D_MODEL), row),
            pl.BlockSpec((TM, LRU_WIDTH), row),
            pl.BlockSpec((TM, SB_WIDTH), row),
            pl.BlockSpec((1, SB_WIDTH), const2),
            pl.BlockSpec(wout.shape, const2, **resident),
            pl.BlockSpec((1, D_MODEL), const2),
            pl.BlockSpec(wup.shape, const2, **resident),
            pl.BlockSpec(wdn.shape, const2, **resident),
            pl.BlockSpec((1, D_MODEL), const2),
        ],
        out_specs=pl.BlockSpec((TM, D_MODEL), row),
        out_shape=jax.ShapeDtypeStruct((m, D_MODEL), F32),
        compiler_params=pltpu.CompilerParams(
            dimension_semantics=("arbitrary",), vmem_limit_bytes=VMEM_LIMIT_BYTES),
        name="mix_out",
    )(x2, ylru2, ysb2, gsb, wout, g2, wup, wdn, gf)


def _block_diag(w):
    nblk, n, _ = w.shape
    eye = jnp.eye(nblk, dtype=w.dtype)
    return (eye[:, None, :, None] * w[:, :, None, :]).reshape(nblk * n, nblk * n)


def kernel(x, norm1_g, w_in, conv_w, conv_b, lru_w_a, lru_b_a, lru_w_x, lru_b_x, lru_lambda,
           lru_out_g, sb_out_g, w_out, norm2_g, w_up, w_down, final_g):
    nb, seq, d = x.shape
    assert w_in.shape[0] == 1, "kernel supports depth 1 only"
    layer = 0
    ylru, q, k, v = _mix_in(
        x, norm1_g[layer].reshape(1, d), w_in[layer].astype(BF16),
        conv_w[layer], conv_b[layer].reshape(1, LRU_WIDTH),
        _block_diag(lru_w_a[layer]).astype(BF16), lru_b_a[layer].reshape(1, LRU_WIDTH),
        _block_diag(lru_w_x[layer]).astype(BF16), lru_b_x[layer].reshape(1, LRU_WIDTH),
        lru_lambda[layer].reshape(1, LRU_WIDTH), lru_out_g[layer].reshape(1, LRU_WIDTH))
    ysb = _attn(q, k, v)
    out = _mix_out(
        x.reshape(nb * seq, d), ylru.reshape(nb * seq, LRU_WIDTH), ysb.reshape(nb * seq, SB_WIDTH),
        sb_out_g[layer].reshape(1, SB_WIDTH), w_out[layer].astype(BF16),
        norm2_g[layer].reshape(1, d),
        w_up[layer].astype(BF16), w_down[layer].astype(BF16),
        final_g.reshape(1, d))
    return out.reshape(nb, seq, d)
```
